```python
import math
import jax
import jax.numpy as jnp
from jax import lax
import numpy as np

D_MODEL = 1024
BATCH = 4
SEQ = 4096
DEPTH = 1

N_META = 16
BLOCK = 128
PAD = BLOCK - N_META
EPS = 1e-6
NEG_INF = -1e30
H_A = 8
DK_A = 128
DV_A = 128
CONV_K = 4
CHUNK = 64
H_B = 8
DH_B = 128
FORGET_BIAS = 2.0
N_KEYS = 128
N_EXPERTS = N_KEYS * N_KEYS
H_P = 8
D_QP = 256
TOPK = 16
PEER_BLOCK = 256

A_QK = H_A * DK_A
A_V = H_A * DV_A
B_W = H_B * DH_B
BRANCH_W = A_V
CONV_CH = 2 * A_QK + A_V
IN_SIZES = (CONV_CH, A_V, H_A, H_A, 3 * B_W, H_B, 2 * D_MODEL)
IN_COLS = 2 * A_QK + 2 * A_V + 2 * H_A + 3 * B_W + H_B + 2 * D_MODEL

kernel_name = 'hybrid_gdn_fox_peer_layer'


def rms_norm(x, w):
    xf = x.astype(jnp.float32)
    y = xf * lax.rsqrt(jnp.mean(xf * xf, axis=-1, keepdims=True) + EPS)
    return (y * w.astype(jnp.float32)).astype(x.dtype)


def l2_norm(x):
    xf = x.astype(jnp.float32)
    return (xf * lax.rsqrt(jnp.sum(xf * xf, axis=-1, keepdims=True) + EPS)).astype(x.dtype)


def pad_front(t):
    return jnp.pad(t, [(0, 0), (PAD, 0)] + [(0, 0)] * (t.ndim - 2))


def split_cols(t, sizes):
    return jnp.split(t, [int(s) for s in np.cumsum(sizes)[:-1]], axis=-1)


def causal_dwconv_silu(x, w):
    y = lax.conv_general_dilated(x, w.astype(x.dtype)[:, None, :], window_strides=(1,),
                                 padding=[(CONV_K - 1, 0)],
                                 dimension_numbers=('NWC', 'WIO', 'NWC'),
                                 feature_group_count=x.shape[-1])
    return jax.nn.silu(y)


def gated_delta_rule(q, k, v, beta, g):
    out_dtype = v.dtype
    B_, Lp, H, _ = k.shape
    DV = v.shape[-1]
    N = Lp // CHUNK

    def to_chunks(t):
        t = t.astype(jnp.float32).reshape((B_, N, CHUNK, H) + t.shape[3:])
        return jnp.swapaxes(t, 2, 3)

    q, k, v, beta, g = (to_chunks(t) for t in (q, k, v, beta, g))
    G = jnp.cumsum(g, axis=-1)
    incl = jnp.tril(jnp.ones((CHUNK, CHUNK), bool))
    strict = jnp.tril(jnp.ones((CHUNK, CHUNK), bool), -1)
    diff = G[..., :, None] - G[..., None, :]
    gamma = jnp.where(incl, jnp.exp(jnp.where(incl, diff, 0.0)), 0.0)
    k_beta = k * beta[..., None]
    a_mat = jnp.where(strict, jnp.einsum('bnhcd,bnhsd->bnhcs', k_beta, k) * gamma, 0.0) \
        + jnp.eye(CHUNK, dtype=jnp.float32)
    rhs = jnp.concatenate([v * beta[..., None], k_beta * jnp.exp(G)[..., None]], axis=-1)
    sol = lax.linalg.triangular_solve(a_mat, rhs, left_side=True, lower=True, unit_diagonal=True)
    u, w = sol[..., :DV], sol[..., DV:]
    attn = jnp.einsum('bnhcd,bnhsd->bnhcs', q, k) * gamma
    q_dec = q * jnp.exp(G)[..., None]
    k_tail = k * jnp.exp(G[..., -1:] - G)[..., None]
    chunk_dec = jnp.exp(G[..., -1])

    def step(S, inp):
        qd, w_c, u_c, at, kt, dec = inp
        v_new = u_c - jnp.einsum('bhcd,bhde->bhce', w_c, S)
        o = jnp.einsum('bhcd,bhde->bhce', qd, S) + jnp.einsum('bhcs,bhse->bhce', at, v_new)
        S = S * dec[..., None, None] + jnp.einsum('bhcd,bhce->bhde', kt, v_new)
        return S, o

    xs = tuple(jnp.moveaxis(t, 1, 0) for t in (q_dec, w, u, attn, k_tail, chunk_dec))
    S0 = jnp.zeros((B_, H, k.shape[-1], DV), jnp.float32)
    _, o = lax.scan(step, S0, xs)
    o = jnp.swapaxes(jnp.moveaxis(o, 0, 1), 2, 3).reshape(B_, Lp, H, DV)
    return o.astype(out_dtype)


def forgetting_attention(q, k, v, log_f):
    Lp = q.shape[1]
    scale = DH_B ** -0.5
    cum = jnp.swapaxes(jnp.cumsum(log_f, axis=1), 1, 2)
    pos = jnp.arange(Lp)
    outs = []
    for i in range(Lp // BLOCK):
        lo, hi = i * BLOCK, (i + 1) * BLOCK
        s = jnp.einsum('bthd,bshd->bhts', q[:, lo:hi], k[:, :hi]).astype(jnp.float32) * scale
        bias = cum[:, :, lo:hi, None] - cum[:, :, None, :hi]
        allowed = (pos[None, :hi] <= pos[lo:hi, None]) & (pos[None, :hi] >= PAD)
        p = jax.nn.softmax(jnp.where(allowed, s + bias, NEG_INF), axis=-1).astype(v.dtype)
        outs.append(jnp.einsum('bhts,bshd->bthd', p, v[:, :hi]))
    return jnp.concatenate(outs, axis=1)


def hybrid_mixer(h, w_in, conv_w, a_log, dt_bias, o_norm_a, q_norm_b, k_norm_b, f_bias,
                 w_branch, w_out):
    B_, L, _ = h.shape
    proj = h @ w_in
    qkv_a, z_a, b_a, a_a, qkv_b, f_b, gates = split_cols(proj, IN_SIZES)
    qkv_a = causal_dwconv_silu(pad_front(qkv_a), conv_w)
    Lp = qkv_a.shape[1]
    qa, ka, va = split_cols(qkv_a, (A_QK, A_QK, A_V))
    qa = l2_norm(qa.reshape(B_, Lp, H_A, DK_A)) * (DK_A ** -0.5)
    ka = l2_norm(ka.reshape(B_, Lp, H_A, DK_A))
    va = va.reshape(B_, Lp, H_A, DV_A)
    beta = pad_front(jax.nn.sigmoid(b_a))
    g = pad_front(-jnp.exp(a_log) * jax.nn.softplus(a_a + dt_bias))
    o_a = gated_delta_rule(qa, ka, va, beta, g)[:, PAD:]
    o_a = rms_norm(o_a, o_norm_a) * jax.nn.silu(z_a.reshape(B_, L, H_A, DV_A))
    qb, kb, vb = split_cols(qkv_b, (B_W, B_W, B_W))
    qb = rms_norm(qb.reshape(B_, L, H_B, DH_B), q_norm_b)
    kb = rms_norm(kb.reshape(B_, L, H_B, DH_B), k_norm_b)
    vb = vb.reshape(B_, L, H_B, DH_B)
    log_f = jax.nn.log_sigmoid((f_b + f_bias).astype(jnp.float32))
    o_b = forgetting_attention(pad_front(qb), pad_front(kb), pad_front(vb), pad_front(log_f))[:, PAD:]
    o = jnp.stack([o_a.reshape(B_, L, BRANCH_W), o_b.reshape(B_, L, BRANCH_W)], axis=2)
    y = jnp.einsum('blgc,gcd->blgd', o, w_branch)
    gate = jax.nn.sigmoid(gates.reshape(B_, L, 2, D_MODEL))
    return jnp.sum(gate * y, axis=2) @ w_out


def peer_ffn(h, wq, sub_keys, expert_u, expert_v):
    B_, L, D = h.shape
    T = B_ * L
    n_blk = -(-T // PEER_BLOCK)
    tokens = jnp.pad(h.reshape(T, D), [(0, n_blk * PEER_BLOCK - T), (0, 0)])
    tokens = tokens.reshape(n_blk, PEER_BLOCK, D)

    def one_block(xb):
        qh = (xb @ wq).reshape(PEER_BLOCK, H_P, 2, D_QP // 2)
        s1 = jnp.einsum('thd,hnd->thn', qh[:, :, 0], sub_keys[0]).astype(jnp.float32)
        s2 = jnp.einsum('thd,hnd->thn', qh[:, :, 1], sub_keys[1]).astype(jnp.float32)
        v1, i1 = lax.top_k(s1, TOPK)
        v2, i2 = lax.top_k(s2, TOPK)
        cand = (v1[..., :, None] + v2[..., None, :]).reshape(PEER_BLOCK, H_P, TOPK * TOPK)
        sc, ci = lax.top_k(cand, TOPK)
        idx = jnp.take_along_axis(i1, ci // TOPK, axis=-1) * N_KEYS \
            + jnp.take_along_axis(i2, ci % TOPK, axis=-1)
        gate_w = jax.nn.softmax(sc, axis=-1)
        act = jax.nn.gelu(jnp.einsum('thkd,td->thk', expert_u[idx], xb).astype(jnp.float32),
                          approximate=False)
        coef = (gate_w * act).astype(xb.dtype)
        return jnp.einsum('thk,thkd->td', coef, expert_v[idx])

    out = lax.map(one_block, tokens)
    return out.reshape(n_blk * PEER_BLOCK, D)[:T].reshape(B_, L, D)


def setup_inputs(seed: int = 0) -> dict:
    key = jax.random.key(seed)
    ks = jax.random.split(key, 18)

    def nrm(k, shape, scale):
        return scale * jax.random.normal(k, shape, jnp.float32)

    dt = jnp.exp(jax.random.uniform(ks[6], (DEPTH, H_A), jnp.float32,
                                    math.log(1e-3), math.log(1e-1)))
    return {
        'x': nrm(ks[0], (BATCH, SEQ, D_MODEL), 1.0),
        'meta_tokens': nrm(ks[1], (N_META, D_MODEL), 1.0),
        'norm_mix': 1.0 + nrm(ks[2], (DEPTH, D_MODEL), 0.02),
        'w_in': nrm(ks[3], (DEPTH, D_MODEL, IN_COLS), D_MODEL ** -0.5),
        'conv_w': nrm(ks[4], (DEPTH, CONV_K, CONV_CH), CONV_K ** -0.5),
        'a_log': jnp.log(jax.random.uniform(ks[5], (DEPTH, H_A), jnp.float32, 1.0, 16.0)),
        'dt_bias': dt + jnp.log(-jnp.expm1(-dt)),
        'o_norm_a': 1.0 + nrm(ks[7], (DEPTH, DV_A), 0.02),
        'q_norm_b': 1.0 + nrm(ks[8], (DEPTH, DH_B), 0.02),
        'k_norm_b': 1.0 + nrm(ks[9], (DEPTH, DH_B), 0.02),
        'f_bias': FORGET_BIAS + nrm(ks[10], (DEPTH, H_B), 0.1),
        'w_branch': nrm(ks[11], (DEPTH, 2, BRANCH_W, D_MODEL), BRANCH_W ** -0.5),
        'w_out': nrm(ks[12], (DEPTH, D_MODEL, D_MODEL), D_MODEL ** -0.5),
        'norm_ffn': 1.0 + nrm(ks[13], (DEPTH, D_MODEL), 0.02),
        'peer_wq': nrm(ks[14], (DEPTH, D_MODEL, H_P * D_QP), D_MODEL ** -0.5),
        'peer_sub_keys': nrm(ks[15], (DEPTH, 2, H_P, N_KEYS, D_QP // 2), (D_QP // 2) ** -0.5),
        'expert_u': nrm(ks[16], (DEPTH, N_EXPERTS, D_MODEL), D_MODEL ** -0.5),
        'expert_v': nrm(ks[17], (DEPTH, N_EXPERTS, D_MODEL), D_MODEL ** -0.5),
    }


def reference(x, meta_tokens, norm_mix, w_in, conv_w, a_log, dt_bias, o_norm_a, q_norm_b,
              k_norm_b, f_bias, w_branch, w_out, norm_ffn, peer_wq, peer_sub_keys,
              expert_u, expert_v):
    B_ = x.shape[0]
    meta = jnp.broadcast_to(meta_tokens.astype(x.dtype)[None], (B_, N_META, D_MODEL))
    res = jnp.concatenate([meta, x], axis=1)
    for layer in range(DEPTH):
        res = res + hybrid_mixer(rms_norm(res, norm_mix[layer]), w_in[layer], conv_w[layer],
                                 a_log[layer], dt_bias[layer], o_norm_a[layer],
                                 q_norm_b[layer], k_norm_b[layer], f_bias[layer],
                                 w_branch[layer], w_out[layer])
        res = res + peer_ffn(rms_norm(res, norm_ffn[layer]), peer_wq[layer],
                             peer_sub_keys[layer], expert_u[layer], expert_v[layer])
    return res[:, N_META:]
```

```python
import functools
import math

import jax
import jax.numpy as jnp
from jax import lax
from jax.experimental import pallas as pl
from jax.experimental.pallas import tpu as pltpu

F32, BF16, I32 = jnp.float32, jnp.bfloat16, jnp.int32
EPS = 1e-6
NEG_INF = -1e30
LANES = 128
BLOCK = 128
CHUNK = 64
CONV_K = 4
TOPK = 16
SMALL_ROWS = 32
VMEM_LIMIT = 56 * 1024 * 1024


def _cparams(sem):
    return pltpu.CompilerParams(dimension_semantics=sem, vmem_limit_bytes=VMEM_LIMIT)


def _pick_tile(n, target, mult):
    best = None
    for t in range(mult, min(n, target) + 1, mult):
        if n % t == 0:
            best = t
    assert best is not None, (n, target, mult)
    return best


def _nt_dot(a, b):
    return lax.dot_general(a, b, (((1,), (1,)), ((), ())), preferred_element_type=F32)


def _tn_dot(a, b):
    return lax.dot_general(a, b, (((0,), (0,)), ((), ())), preferred_element_type=F32)


def _split3(x):
    hi = x.astype(BF16)
    r = x - hi.astype(F32)
    mid = r.astype(BF16)
    lo = (r - mid.astype(F32)).astype(BF16)
    return hi, mid, lo


def _dot_hi(a, b):
    a_hi = a.astype(BF16)
    a_lo = (a - a_hi.astype(F32)).astype(BF16)
    b_hi = b.astype(BF16)
    b_lo = (b - b_hi.astype(F32)).astype(BF16)
    d = functools.partial(jnp.dot, preferred_element_type=F32)
    return d(a_hi, b_hi) + (d(a_hi, b_lo) + d(a_lo, b_hi))


def _dot_01(x, m01):
    hi, mid, lo = _split3(x)
    d = functools.partial(jnp.dot, preferred_element_type=F32)
    return d(hi, m01) + (d(mid, m01) + d(lo, m01))


def _sigmoid(x):
    return 1.0 / (1.0 + jnp.exp(-x))


def _softplus(x):
    return jnp.maximum(x, 0.0) + jnp.log1p(jnp.exp(-jnp.abs(x)))


def _rmsnorm_body(x_ref, w_ref, o_ref):
    x = x_ref[...].astype(F32)
    y = x * lax.rsqrt(jnp.mean(x * x, axis=-1, keepdims=True) + EPS)
    o_ref[...] = (y * w_ref[...]).astype(o_ref.dtype)


def _rmsnorm(x2d, w, out_dtype):
    m, d = x2d.shape
    tm = _pick_tile(m, 1024, 8)
    return pl.pallas_call(
        _rmsnorm_body,
        grid=(m // tm,),
        in_specs=[pl.BlockSpec((tm, d), lambda i: (i, 0)), pl.BlockSpec((1, d), lambda i: (0, 0))],
        out_specs=pl.BlockSpec((tm, d), lambda i: (i, 0)),
        out_shape=jax.ShapeDtypeStruct((m, d), out_dtype),
        compiler_params=_cparams(("parallel",)),
        name="rmsnorm",
    )(x2d, w.reshape(1, d).astype(F32))


def _proj_plain_body(x_ref, w_ref, o_ref):
    o_ref[...] = jnp.dot(x_ref[...], w_ref[...], preferred_element_type=F32).astype(o_ref.dtype)


def _proj_silu_body(x_ref, w_ref, o_ref):
    y = jnp.dot(x_ref[...], w_ref[...], preferred_element_type=F32)
    o_ref[...] = (y * _sigmoid(y)).astype(o_ref.dtype)


def _proj_sigmoid_body(x_ref, w_ref, o_ref):
    y = jnp.dot(x_ref[...], w_ref[...], preferred_element_type=F32)
    o_ref[...] = _sigmoid(y).astype(o_ref.dtype)


def _proj_qkvb_body(x_ref, w_ref, nw_ref, o_ref):
    y = jnp.dot(x_ref[...], w_ref[...], preferred_element_type=F32)
    do_norm = pl.program_id(1) < 2
    for h in range(y.shape[1] // LANES):
        sl = slice(h * LANES, (h + 1) * LANES)
        seg = y[:, sl]
        r = lax.rsqrt(jnp.mean(seg * seg, axis=-1, keepdims=True) + EPS)
        r = jnp.where(do_norm, r, 1.0)
        o_ref[:, sl] = (seg * r * nw_ref[:, sl]).astype(o_ref.dtype)


def _proj(body, x2d, w, out_dtype, tn, extra=(), name="proj"):
    m, k = x2d.shape
    n = w.shape[1]
    tm = _pick_tile(m, 1536, 128)
    assert n % tn == 0
    in_specs = [pl.BlockSpec((tm, k), lambda i, j: (i, 0)), pl.BlockSpec((k, tn), lambda i, j: (0, j))]
    in_specs += [pl.BlockSpec((1, tn), lambda i, j: (0, j)) for _ in extra]
    return pl.pallas_call(
        body,
        grid=(m // tm, n // tn),
        in_specs=in_specs,
        out_specs=pl.BlockSpec((tm, tn), lambda i, j: (i, j)),
        out_shape=jax.ShapeDtypeStruct((m, n), out_dtype),
        compiler_params=_cparams(("parallel", "arbitrary")),
        name=name,
    )(x2d, w, *extra)


def _small_body(w_ref, x_ref, o_ref):
    o_ref[...] = _nt_dot(w_ref[...], x_ref[...])


def _proj_small_t(x2d, w_t):
    m, k = x2d.shape
    tm = _pick_tile(m, 2048, 128)
    return pl.pallas_call(
        _small_body,
        grid=(m // tm,),
        in_specs=[pl.BlockSpec((SMALL_ROWS, k), lambda i: (0, 0)), pl.BlockSpec((tm, k), lambda i: (i, 0))],
        out_specs=pl.BlockSpec((SMALL_ROWS, tm), lambda i: (0, i)),
        out_shape=jax.ShapeDtypeStruct((SMALL_ROWS, m), F32),
        compiler_params=_cparams(("parallel",)),
        name="proj_small",
    )(w_t, x2d)


def _prep_body(s_ref, alog_ref, dtb_ref, fb_ref, beta_ref, gc_ref, cumf_ref, carry_ref, *, pad):
    j = pl.program_id(1)

    @pl.when(j == 0)
    def _():
        carry_ref[...] = jnp.zeros_like(carry_ref)

    sm = s_ref[...]
    b, a, f = sm[0:8], sm[8:16], sm[16:24]
    pos = j * LANES + lax.broadcasted_iota(I32, (8, LANES), 1)
    valid = pos >= pad
    beta = jnp.where(valid, _sigmoid(b), 0.0)
    g = jnp.where(valid, -jnp.exp(alog_ref[...]) * _softplus(a + dtb_ref[...]), 0.0)
    logf = jnp.where(valid, -_softplus(-(f + fb_ref[...])), 0.0)
    r = lax.broadcasted_iota(I32, (LANES, LANES), 0)
    c = lax.broadcasted_iota(I32, (LANES, LANES), 1)
    upper = r <= c
    shift = CHUNK.bit_length() - 1
    same_chunk = lax.shift_right_logical(r, shift) == lax.shift_right_logical(c, shift)
    m_seq = jnp.where(upper, 1.0, 0.0).astype(BF16)
    m_chunk = jnp.where(upper & same_chunk, 1.0, 0.0).astype(BF16)
    beta_ref[0] = beta
    gc_ref[0] = _dot_01(g, m_chunk)
    cum = _dot_01(logf, m_seq) + carry_ref[...]
    cumf_ref[0] = cum
    carry_ref[...] = jnp.broadcast_to(cum[:, LANES - 1:LANES], (8, LANES))


def _prep(small_t, alog, dtb, fb, batch, lp, pad):
    nb = lp // LANES
    out = jax.ShapeDtypeStruct((batch, 8, lp), F32)
    pspec = pl.BlockSpec((8, LANES), lambda b, j: (0, 0))
    ospec = pl.BlockSpec((1, 8, LANES), lambda b, j: (b, 0, j))
    return pl.pallas_call(
        functools.partial(_prep_body, pad=pad),
        grid=(batch, nb),
        in_specs=[pl.BlockSpec((SMALL_ROWS, LANES), lambda b, j: (0, b * nb + j)), pspec, pspec, pspec],
        out_specs=[ospec, ospec, ospec],
        out_shape=[out, out, out],
        scratch_shapes=[pltpu.VMEM((8, LANES), F32)],
        compiler_params=_cparams(("parallel", "arbitrary")),
        name="gate_prep",
    )(small_t, alog, dtb, fb)


def _gdn_body(xq_ref, xk_ref, xv_ref, cwq_ref, cwk_ref, cwv_ref, beta_ref, gc_ref, zs_ref, onw_ref,
              o_ref, s_ref, u_s, w_s, qd_s, kt_s, at_s, dec_s, *, n_chunks, group):
    dk = xq_ref.shape[-1]
    ii = lax.broadcasted_iota(I32, (CHUNK, CHUNK), 0)
    jj = lax.broadcasted_iota(I32, (CHUNK, CHUNK), 1)
    eye = ii == jj
    incl = ii >= jj
    strict = ii > jj
    eye_f = jnp.where(eye, 1.0, 0.0)

    def to_col(row):
        return jnp.sum(jnp.where(eye, row, 0.0), axis=1, keepdims=True)

    def conv_silu(x_ref, cw_ref, a0):
        win = x_ref[0, pl.ds(a0, CHUNK + 8), :]
        cw = cw_ref[...]
        y = cw[0:1] * win[5:5 + CHUNK]
        for i in range(1, CONV_K):
            y = y + cw[i:i + 1] * win[5 + i:5 + i + CHUNK]
        return y * _sigmoid(y)

    def wy_chunk(c):
        r0 = pl.multiple_of(c * CHUNK, CHUNK)
        a0 = pl.multiple_of(c * CHUNK - 8, 8)
        q = conv_silu(xq_ref, cwq_ref, a0)
        k = conv_silu(xk_ref, cwk_ref, a0)
        v = conv_silu(xv_ref, cwv_ref, a0)
        q = q * lax.rsqrt(jnp.sum(q * q, axis=-1, keepdims=True) + EPS) * (dk ** -0.5)
        k = k * lax.rsqrt(jnp.sum(k * k, axis=-1, keepdims=True) + EPS)
        b_row = beta_ref[0, pl.ds(c, 1), :]
        g_row = gc_ref[0, pl.ds(c, 1), :]
        b_col = to_col(b_row)
        g_col = to_col(g_row)
        diff = g_col - g_row
        gamma = jnp.where(incl, jnp.exp(jnp.where(incl, diff, 0.0)), 0.0)
        kb = k * b_col
        kf = k.astype(BF16)
        a_mat = jnp.where(strict, _nt_dot(kb.astype(BF16), kf) * gamma, 0.0)
        x = -a_mat
        t = eye_f + x
        for _ in range(5):
            x = _dot_hi(x, x)
            t = t + _dot_hi(t, x)
        e_g = jnp.exp(g_col)
        sol = _dot_hi(t, jnp.concatenate([v * b_col, kb * e_g], axis=1))
        g_last = g_row[:, CHUNK - 1:CHUNK]
        u_s[pl.ds(r0, CHUNK), :] = sol[:, :dk]
        w_s[pl.ds(r0, CHUNK), :] = sol[:, dk:].astype(BF16)
        qd_s[pl.ds(r0, CHUNK), :] = (q * e_g).astype(BF16)
        kt_s[pl.ds(r0, CHUNK), :] = (k * jnp.exp(g_last - g_col)).astype(BF16)
        at_s[pl.ds(r0, CHUNK), :] = (_nt_dot(q.astype(BF16), kf) * gamma).astype(BF16)
        dec_s[pl.ds(c, 1), :] = jnp.broadcast_to(jnp.exp(g_last), (1, LANES))

    def wy_group(gi, carry):
        for i in range(group):
            wy_chunk(1 + gi * group + i)
        return carry

    lax.fori_loop(0, (n_chunks - 1) // group, wy_group, 0)

    s_ref[...] = jnp.zeros_like(s_ref)

    def scan_chunk(c, store):
        r0 = pl.multiple_of(c * CHUNK, CHUNK)
        s = s_ref[...]
        sb = s.astype(BF16)
        ws = jnp.dot(jnp.concatenate([w_s[pl.ds(r0, CHUNK), :], qd_s[pl.ds(r0, CHUNK), :]], axis=0), sb,
                     preferred_element_type=F32)
        v_new = u_s[pl.ds(r0, CHUNK), :] - ws[:CHUNK]
        vb = v_new.astype(BF16)
        o = ws[CHUNK:] + jnp.dot(at_s[pl.ds(r0, CHUNK), :], vb, preferred_element_type=F32)
        s_ref[...] = s * dec_s[pl.ds(c, 1), :] + _tn_dot(kt_s[pl.ds(r0, CHUNK), :], vb)
        if store:
            ro = pl.multiple_of(r0 - BLOCK, CHUNK)
            on = o * lax.rsqrt(jnp.mean(o * o, axis=-1, keepdims=True) + EPS) * onw_ref[...]
            o_ref[0, pl.ds(ro, CHUNK), :] = (on * zs_ref[0, pl.ds(ro, CHUNK), :].astype(F32)).astype(o_ref.dtype)

    scan_chunk(1, False)

    def scan_step(c, carry):
        scan_chunk(c, True)
        return carry

    lax.fori_loop(BLOCK // CHUNK, n_chunks, scan_step, 0)


def _gdn(qkv_a, conv_w, beta3, gc3, zs, onw, batch, lp, heads):
    seq = lp - BLOCK
    n_chunks = lp // CHUNK
    group = max(g for g in (5, 4, 3, 2, 1) if (n_chunks - 1) % g == 0)
    xspec = lambda off: pl.BlockSpec((1, lp, LANES), lambda b, h: (b, 0, off + h))
    cspec = lambda off: pl.BlockSpec((CONV_K, LANES), lambda b, h: (0, off + h))
    rspec = pl.BlockSpec((1, n_chunks, CHUNK), lambda b, h: (b * 8 + h, 0, 0))
    ospec = pl.BlockSpec((1, seq, LANES), lambda b, h: (b, 0, h))
    return pl.pallas_call(
        functools.partial(_gdn_body, n_chunks=n_chunks, group=group),
        grid=(batch, heads),
        in_specs=[xspec(0), xspec(heads), xspec(2 * heads), cspec(0), cspec(heads), cspec(2 * heads),
                  rspec, rspec, ospec, pl.BlockSpec((1, LANES), lambda b, h: (0, 0))],
        out_specs=ospec,
        out_shape=jax.ShapeDtypeStruct((batch, seq, heads * LANES), BF16),
        scratch_shapes=[pltpu.VMEM((LANES, LANES), F32), pltpu.VMEM((lp, LANES), F32),
                        pltpu.VMEM((lp, LANES), BF16), pltpu.VMEM((lp, LANES), BF16),
                        pltpu.VMEM((lp, LANES), BF16), pltpu.VMEM((lp, CHUNK), BF16),
                        pltpu.VMEM((n_chunks, LANES), F32)],
        compiler_params=_cparams(("parallel", "parallel")),
        name="gated_delta_rule",
    )(qkv_a, qkv_a, qkv_a, conv_w, conv_w, conv_w, beta3, gc3, zs, onw)


def _fox_body(q_ref, k_ref, v_ref, cum_ref, o_ref, *, tq, pad):
    i = pl.program_id(2)
    q0 = pl.multiple_of(BLOCK + i * tq, BLOCK)
    q = q_ref[0, pl.ds(q0, tq), :]
    ii = lax.broadcasted_iota(I32, (tq, tq), 0)
    jj = lax.broadcasted_iota(I32, (tq, tq), 1)
    cq = jnp.sum(jnp.where(ii == jj, cum_ref[0, :, pl.ds(q0, tq)], 0.0), axis=1, keepdims=True)

    def tile(k0, size, mask, carry):
        m, l, acc = carry
        s = _nt_dot(q, k_ref[0, pl.ds(k0, size), :]) + (cq - cum_ref[0, :, pl.ds(k0, size)])
        if mask is not None:
            s = jnp.where(mask, s, NEG_INF)
        m_new = jnp.maximum(m, jnp.max(s, axis=1, keepdims=True))
        alpha = jnp.exp(m - m_new)
        p = jnp.exp(s - m_new)
        l = alpha * l + jnp.sum(p, axis=1, keepdims=True)
        acc = alpha * acc + jnp.dot(p.astype(BF16), v_ref[0, pl.ds(k0, size), :], preferred_element_type=F32)
        return m_new, l, acc

    carry = (jnp.full((tq, 1), NEG_INF, F32), jnp.zeros((tq, 1), F32), jnp.zeros((tq, LANES), F32))
    carry = tile(0, BLOCK, lax.broadcasted_iota(I32, (tq, BLOCK), 1) >= pad, carry)
    carry = lax.fori_loop(0, i, lambda j, c: tile(pl.multiple_of(BLOCK + j * tq, BLOCK), tq, None, c), carry)
    m, l, acc = tile(q0, tq, jj <= ii, carry)
    o_ref[0] = (acc / l).astype(o_ref.dtype)


def _fox(qkv, cumf3, batch, lp, heads, pad):
    seq = lp - BLOCK
    tq = _pick_tile(seq, 512, BLOCK)
    kvspec = lambda off: pl.BlockSpec((1, lp, LANES), lambda b, h, i: (b, 0, off + h))
    return pl.pallas_call(
        functools.partial(_fox_body, tq=tq, pad=pad),
        grid=(batch, heads, seq // tq),
        in_specs=[kvspec(0), kvspec(heads), kvspec(2 * heads),
                  pl.BlockSpec((1, 1, lp), lambda b, h, i: (b * 8 + h, 0, 0))],
        out_specs=pl.BlockSpec((1, tq, LANES), lambda b, h, i: (b, i, h)),
        out_shape=jax.ShapeDtypeStruct((batch, seq, heads * LANES), BF16),
        compiler_params=_cparams(("parallel", "parallel", "arbitrary")),
        name="forgetting_attention",
    )(qkv, qkv, qkv, cumf3)


def _merge_body(oa_ref, ob_ref, g_ref, x_ref, wa_ref, wb_ref, wo_ref, nw_ref, res_ref, xn_ref):
    d = x_ref.shape[1]
    ya = jnp.dot(oa_ref[...], wa_ref[...], preferred_element_type=F32)
    yb = jnp.dot(ob_ref[...], wb_ref[...], preferred_element_type=F32)
    g = g_ref[...].astype(F32)
    mix = g[:, :d] * ya + g[:, d:] * yb
    res = x_ref[...] + jnp.dot(mix.astype(BF16), wo_ref[...], preferred_element_type=F32)
    res_ref[...] = res
    xn = res * lax.rsqrt(jnp.mean(res * res, axis=-1, keepdims=True) + EPS) * nw_ref[...]
    xn_ref[...] = xn.astype(xn_ref.dtype)


def _merge(oa, ob, gates, x2d, wa, wb, wo, nw):
    m, d = x2d.shape
    bw = oa.shape[1]
    tm = _pick_tile(m, 512, 128)
    row = lambda c: pl.BlockSpec((tm, c), lambda i: (i, 0))
    full = lambda r, c: pl.BlockSpec((r, c), lambda i: (0, 0))
    return pl.pallas_call(
        _merge_body,
        grid=(m // tm,),
        in_specs=[row(bw), row(bw), row(2 * d), row(d), full(bw, d), full(bw, d), full(d, d), full(1, d)],
        out_specs=[row(d), row(d)],
        out_shape=[jax.ShapeDtypeStruct((m, d), F32), jax.ShapeDtypeStruct((m, d), BF16)],
        compiler_params=_cparams(("parallel",)),
        name="merge_out_proj",
    )(oa, ob, gates, x2d, wa, wb, wo, nw)


_CAND_GROUPS = ((0, 0, 8, 8), (0, 8, 16, 8), (1, 0, 8, 8), (2, 0, 8, 5), (3, 0, 8, 4), (4, 0, 8, 3),
                (5, 0, 8, 2), (6, 0, 8, 2), (7, 0, 8, 2))


def _top_rows(x, key, n, val_ref, key_ref=None, payloads=()):
    big = jnp.float32(1e9)
    for kk in range(n):
        m = jnp.max(x, axis=0, keepdims=True)
        pos = jnp.min(jnp.where(x == m, key, big), axis=0, keepdims=True)
        sel = key == pos
        val_ref[kk:kk + 1, :] = m
        if key_ref is not None:
            key_ref[kk:kk + 1, :] = pos
        for ref, pay in payloads:
            ref[kk:kk + 1, :] = jnp.sum(jnp.where(sel, pay, 0.0), axis=0, keepdims=True)
        x = jnp.where(sel, -jnp.inf, x)


def _route_body(xn_ref, wqt_ref, keys_ref, g_ref, a_ref, b_ref, qt_s, v1_s, i1_s, v2_s, i2_s,
                sc_s, ea_s, eb_s, gt_s, at_s, bt_s, *, heads, n_keys):
    tm = xn_ref.shape[0]
    qt_s[...] = _nt_dot(wqt_ref[...], xn_ref[...]).astype(BF16)
    key_iota = lax.broadcasted_iota(I32, (n_keys, LANES), 0).astype(F32)
    sub = lax.broadcasted_iota(I32, (8, LANES), 0)

    def head(h, carry):
        for half in range(tm // LANES):
            ls = slice(half * LANES, (half + 1) * LANES)
            for p, (v_s, i_s) in enumerate(((v1_s, i1_s), (v2_s, i2_s))):
                qhp = qt_s[pl.ds(pl.multiple_of(h * 2 * LANES + p * LANES, LANES), LANES), ls]
                kk = keys_ref[pl.ds(pl.multiple_of((p * heads + h) * n_keys, n_keys), n_keys), :]
                st = jnp.dot(kk, qhp, preferred_element_type=F32)
                _top_rows(st, key_iota, TOPK, v_s, i_s)
            v1, i1, v2, i2 = v1_s[...], i1_s[...], v2_s[...], i2_s[...]
            cands, flats, cas, cbs = [], [], [], []
            for (a, b0, b1, nvalid) in _CAND_GROUPS:
                ok = sub < nvalid
                cands.append(jnp.where(ok, v1[a:a + 1] + v2[b0:b1], -jnp.inf))
                flats.append((a * TOPK + b0 + sub).astype(F32))
                cas.append(jnp.broadcast_to(i1[a:a + 1], (8, LANES)))
                cbs.append(i2[b0:b1])
            cands.append(v1[8:16] + v2[0:1])
            flats.append(((8 + sub) * TOPK).astype(F32))
            cas.append(i1[8:16])
            cbs.append(jnp.broadcast_to(i2[0:1], (8, LANES)))
            cat = lambda xs: jnp.concatenate(xs, axis=0)
            _top_rows(cat(cands), cat(flats), TOPK, sc_s, payloads=((ea_s, cat(cas)), (eb_s, cat(cbs))))
            sc = sc_s[...]
            e = jnp.exp(sc - sc[0:1])
            row = pl.ds(pl.multiple_of(h * TOPK, TOPK), TOPK)
            gt_s[row, ls] = e / jnp.sum(e, axis=0, keepdims=True)
            at_s[row, ls] = ea_s[...]
            bt_s[row, ls] = eb_s[...]
        return carry

    lax.fori_loop(0, heads, head, 0)
    g_ref[...] = gt_s[...].T
    a_ref[...] = at_s[...].T
    b_ref[...] = bt_s[...].T


def _route(xn, wq_t, keys2d, heads, n_keys):
    t, d = xn.shape
    tm = _pick_tile(t, 256, LANES)
    hk = heads * TOPK
    out = jax.ShapeDtypeStruct((t, hk), F32)
    ospec = pl.BlockSpec((tm, hk), lambda i: (i, 0))
    small = lambda: pltpu.VMEM((TOPK, LANES), F32)
    return pl.pallas_call(
        functools.partial(_route_body, heads=heads, n_keys=n_keys),
        grid=(t // tm,),
        in_specs=[pl.BlockSpec((tm, d), lambda i: (i, 0)),
                  pl.BlockSpec(wq_t.shape, lambda i: (0, 0)),
                  pl.BlockSpec(keys2d.shape, lambda i: (0, 0))],
        out_specs=[ospec, ospec, ospec],
        out_shape=[out, out, out],
        scratch_shapes=[pltpu.VMEM((wq_t.shape[0], tm), BF16)] + [small() for _ in range(7)]
        + [pltpu.VMEM((hk, tm), F32) for _ in range(3)],
        compiler_params=_cparams(("parallel",)),
        name="peer_route",
    )(xn, wq_t, keys2d)


def _gelu(x):
    return 0.5 * x * (1.0 + lax.erf(x * (2.0 ** -0.5)))


def _expert_body(xn_ref, res_ref, g_ref, a_ref, b_ref, ut_ref, v_ref, o_ref, w_s, acc_s, *, n_keys):
    j = pl.program_id(1)
    tm = xn_ref.shape[0]
    te = ut_ref.shape[1]

    @pl.when(j == 0)
    def _():
        acc_s[...] = jnp.zeros_like(acc_s)
        key = lax.broadcasted_iota(I32, (n_keys, g_ref.shape[1]), 0).astype(F32)

        def token(t, carry):
            pt = jnp.where(key == a_ref[pl.ds(t, 1), :], g_ref[pl.ds(t, 1), :], 0.0)
            pt_hi = pt.astype(BF16)
            pt_lo = (pt - pt_hi.astype(F32)).astype(BF16)
            qt = jnp.where(key == b_ref[pl.ds(t, 1), :], 1.0, 0.0).astype(BF16)
            w_t = _nt_dot(jnp.concatenate([pt_hi, pt_lo], axis=1), jnp.concatenate([qt, qt], axis=1))
            w_s[pl.ds(t, n_keys, stride=tm), :] = w_t
            return carry

        lax.fori_loop(0, tm, token, 0, unroll=4)

    act = jnp.dot(xn_ref[...], ut_ref[...], preferred_element_type=F32)
    coef = []
    for s in range(te // n_keys):
        w0 = pl.multiple_of((j * (te // n_keys) + s) * tm, tm)
        coef.append(_gelu(act[:, s * n_keys:(s + 1) * n_keys]) * w_s[pl.ds(w0, tm), :])
    acc_s[...] += jnp.dot(jnp.concatenate(coef, axis=1).astype(BF16), v_ref[...], preferred_element_type=F32)

    @pl.when(j == pl.num_programs(1) - 1)
    def _():
        o_ref[...] = res_ref[...] + acc_s[...]


def _experts(xn, res, g, a, b, u_t, v, n_keys):
    t, d = xn.shape
    ne = v.shape[0]
    hk = g.shape[1]
    tm = _pick_tile(t, 256, LANES)
    te = _pick_tile(ne, 512, n_keys)
    tok = lambda c: pl.BlockSpec((tm, c), lambda i, j: (i, 0))
    return pl.pallas_call(
        functools.partial(_expert_body, n_keys=n_keys),
        grid=(t // tm, ne // te),
        in_specs=[tok(d), tok(d), tok(hk), tok(hk), tok(hk),
                  pl.BlockSpec((d, te), lambda i, j: (0, j)), pl.BlockSpec((te, d), lambda i, j: (j, 0))],
        out_specs=tok(d),
        out_shape=jax.ShapeDtypeStruct((t, d), F32),
        scratch_shapes=[pltpu.VMEM((n_keys * tm, n_keys), F32), pltpu.VMEM((tm, d), F32)],
        compiler_params=_cparams(("parallel", "arbitrary")),
        name="peer_experts",
    )(xn, res, g, a, b, u_t, v)


def _layer(x, meta_tokens, norm_mix, w_in, conv_w, a_log, dt_bias, o_norm_a, q_norm_b, k_norm_b, f_bias,
           w_branch, w_out, norm_ffn, peer_wq, peer_sub_keys, expert_u, expert_v):
    batch, seq, d = x.shape
    n_meta = meta_tokens.shape[0]
    pad = BLOCK - n_meta
    lp = BLOCK + seq
    ha, hb = a_log.shape[0], f_bias.shape[0]
    aqk = ha * LANES
    bw = hb * LANES
    assert ha <= 8 and hb <= 8 and seq % BLOCK == 0 and conv_w.shape == (CONV_K, 3 * aqk)
    assert w_in.shape[1] == 4 * aqk + 2 * ha + 3 * bw + hb + 2 * d
    c_z, c_b, c_a = 3 * aqk, 4 * aqk, 4 * aqk + ha
    c_qkvb = 4 * aqk + 2 * ha
    c_f = c_qkvb + 3 * bw
    c_g = c_f + hb

    x2d = x.reshape(batch * seq, d)
    hn_x = _rmsnorm(x2d, norm_mix, BF16)
    hn_m = _rmsnorm(meta_tokens, norm_mix, BF16)
    hn_p = jnp.concatenate([jnp.zeros((batch, pad, d), BF16), jnp.broadcast_to(hn_m[None], (batch, n_meta, d)),
                            hn_x.reshape(batch, seq, d)], axis=1).reshape(batch * lp, d)

    wb16 = w_in.astype(BF16)
    qkv_a = _proj(_proj_plain_body, hn_p, wb16[:, :c_z], F32, _pick_tile(c_z, 1024, LANES), name="proj_qkv_a")
    zs = _proj(_proj_silu_body, hn_x, wb16[:, c_z:c_b], BF16, _pick_tile(aqk, 1024, LANES), name="proj_z")
    scale = LANES ** -0.5
    nw = jnp.concatenate([jnp.tile(q_norm_b * scale, hb), jnp.tile(k_norm_b, hb), jnp.ones((bw,), F32)]).reshape(1, 3 * bw)
    qkv_b = _proj(_proj_qkvb_body, hn_p, wb16[:, c_qkvb:c_f], BF16, bw, extra=(nw,), name="proj_qkv_b")
    gates = _proj(_proj_sigmoid_body, hn_x, wb16[:, c_g:], BF16, _pick_tile(2 * d, 1024, LANES), name="proj_gates")

    def rows8(w):
        return jnp.pad(w.T, ((0, 8 - w.shape[1]), (0, 0)))

    w_small = jnp.concatenate([rows8(wb16[:, c_b:c_a]), rows8(wb16[:, c_a:c_qkvb]), rows8(wb16[:, c_f:c_g]),
                               jnp.zeros((SMALL_ROWS - 24, d), BF16)], axis=0)
    small_t = _proj_small_t(hn_p, w_small)

    def lanes8(p):
        return jnp.broadcast_to(jnp.pad(p.astype(F32), (0, 8 - p.shape[0]))[:, None], (8, LANES))

    beta, gc, cumf = _prep(small_t, lanes8(a_log), lanes8(dt_bias), lanes8(f_bias), batch, lp, pad)
    n_chunks = lp // CHUNK
    o_a = _gdn(qkv_a.reshape(batch, lp, 3 * aqk), conv_w, beta.reshape(batch * 8, n_chunks, CHUNK),
               gc.reshape(batch * 8, n_chunks, CHUNK), zs.reshape(batch, seq, aqk),
               o_norm_a.reshape(1, LANES).astype(F32), batch, lp, ha)

    o_b = _fox(qkv_b.reshape(batch, lp, 3 * bw), cumf.reshape(batch * 8, 1, lp), batch, lp, hb, pad)

    res, xn = _merge(o_a.reshape(batch * seq, aqk), o_b.reshape(batch * seq, bw), gates, x2d,
                     w_branch[0].astype(BF16), w_branch[1].astype(BF16), w_out.astype(BF16),
                     norm_ffn.reshape(1, d).astype(F32))

    hp, n_keys = peer_sub_keys.shape[1], peer_sub_keys.shape[2]
    assert peer_sub_keys.shape[3] == LANES and n_keys == LANES and peer_wq.shape[1] == hp * 2 * LANES
    g_w, a_k, b_k = _route(xn, peer_wq.T.astype(BF16), peer_sub_keys.reshape(2 * hp * n_keys, LANES).astype(BF16),
                           hp, n_keys)
    out = _experts(xn, res, g_w, a_k, b_k, expert_u.T.astype(BF16), expert_v.astype(BF16), n_keys)
    return out.reshape(batch, seq, d)


def kernel(x, meta_tokens, norm_mix, w_in, conv_w, a_log, dt_bias, o_norm_a, q_norm_b, k_norm_b, f_bias,
           w_branch, w_out, norm_ffn, peer_wq, peer_sub_keys, expert_u, expert_v):
    assert norm_mix.shape[0] == 1, "single-layer block"
    return _layer(x, meta_tokens, norm_mix[0], w_in[0], conv_w[0], a_log[0], dt_bias[0], o_norm_a[0],
                  q_norm_b[0], k_norm_b[0], f_bias[0], w_branch[0], w_out[0], norm_ffn[0], peer_wq[0],
                  peer_sub_keys[0], expert_u[0], expert_v[0])
```

```python
import functools
import math

import jax
import jax.numpy as jnp
from jax import lax
from jax.experimental import pallas as pl
from jax.experimental.pallas import tpu as pltpu

F32, BF16, I32 = jnp.float32, jnp.bfloat16, jnp.int32
EPS = 1e-6
NEG_INF = -1e30
LANES = 128
BLOCK = 128
CHUNK = 64
CONV_K = 4
TOPK = 16
SMALL_ROWS = 32
VMEM_LIMIT = 56 * 1024 * 1024


def _cparams(sem):
    return pltpu.CompilerParams(dimension_semantics=sem, vmem_limit_bytes=VMEM_LIMIT)


def _pick_tile(n, target, mult):
    best = None
    for t in range(mult, min(n, target) + 1, mult):
        if n % t == 0:
            best = t
    assert best is not None, (n, target, mult)
    return best


def _nt_dot(a, b):
    return lax.dot_general(a, b, (((1,), (1,)), ((), ())), preferred_element_type=F32)


def _tn_dot(a, b):
    return lax.dot_general(a, b, (((0,), (0,)), ((), ())), preferred_element_type=F32)


def _split3(x):
    hi = x.astype(BF16)
    r = x - hi.astype(F32)
    mid = r.astype(BF16)
    lo = (r - mid.astype(F32)).astype(BF16)
    return hi, mid, lo


def _dot_01(x, m01):
    hi, mid, lo = _split3(x)
    d = functools.partial(jnp.dot, preferred_element_type=F32)
    return d(hi, m01) + (d(mid, m01) + d(lo, m01))


def _sigmoid(x):
    return 1.0 / (1.0 + jnp.exp(-x))


def _softplus(x):
    return jnp.maximum(x, 0.0) + jnp.log1p(jnp.exp(-jnp.abs(x)))


def _rmsnorm_body(x_ref, w_ref, o_ref):
    x = x_ref[...].astype(F32)
    y = x * lax.rsqrt(jnp.mean(x * x, axis=-1, keepdims=True) + EPS)
    o_ref[...] = (y * w_ref[...]).astype(o_ref.dtype)


def _rmsnorm(x2d, w, out_dtype):
    m, d = x2d.shape
    tm = _pick_tile(m, 1024, 8)
    return pl.pallas_call(
        _rmsnorm_body,
        grid=(m // tm,),
        in_specs=[pl.BlockSpec((tm, d), lambda i: (i, 0)), pl.BlockSpec((1, d), lambda i: (0, 0))],
        out_specs=pl.BlockSpec((tm, d), lambda i: (i, 0)),
        out_shape=jax.ShapeDtypeStruct((m, d), out_dtype),
        compiler_params=_cparams(("parallel",)),
        name="rmsnorm",
    )(x2d, w.reshape(1, d).astype(F32))


def _proj_plain_body(x_ref, w_ref, o_ref):
    o_ref[...] = jnp.dot(x_ref[...], w_ref[...], preferred_element_type=F32).astype(o_ref.dtype)


def _proj_silu_body(x_ref, w_ref, o_ref):
    y = jnp.dot(x_ref[...], w_ref[...], preferred_element_type=F32)
    o_ref[...] = (y * _sigmoid(y)).astype(o_ref.dtype)


def _proj_sigmoid_body(x_ref, w_ref, o_ref):
    y = jnp.dot(x_ref[...], w_ref[...], preferred_element_type=F32)
    o_ref[...] = _sigmoid(y).astype(o_ref.dtype)


def _proj_qkvb_body(x_ref, w_ref, nw_ref, o_ref):
    y = jnp.dot(x_ref[...], w_ref[...], preferred_element_type=F32)
    do_norm = pl.program_id(1) < 2
    for h in range(y.shape[1] // LANES):
        sl = slice(h * LANES, (h + 1) * LANES)
        seg = y[:, sl]
        r = lax.rsqrt(jnp.mean(seg * seg, axis=-1, keepdims=True) + EPS)
        r = jnp.where(do_norm, r, 1.0)
        o_ref[:, sl] = (seg * r * nw_ref[:, sl]).astype(o_ref.dtype)


def _proj(body, x2d, w, out_dtype, tn, extra=(), name="proj"):
    m, k = x2d.shape
    n = w.shape[1]
    tm = _pick_tile(m, 1536, 128)
    assert n % tn == 0
    in_specs = [pl.BlockSpec((tm, k), lambda i, j: (i, 0)), pl.BlockSpec((k, tn), lambda i, j: (0, j))]
    in_specs += [pl.BlockSpec((1, tn), lambda i, j: (0, j)) for _ in extra]
    return pl.pallas_call(
        body,
        grid=(m // tm, n // tn),
        in_specs=in_specs,
        out_specs=pl.BlockSpec((tm, tn), lambda i, j: (i, j)),
        out_shape=jax.ShapeDtypeStruct((m, n), out_dtype),
        compiler_params=_cparams(("parallel", "arbitrary")),
        name=name,
    )(x2d, w, *extra)


def _small_body(w_ref, x_ref, o_ref):
    o_ref[...] = _nt_dot(w_ref[...], x_ref[...])


def _proj_small_t(x2d, w_t):
    m, k = x2d.shape
    tm = _pick_tile(m, 2048, 128)
    return pl.pallas_call(
        _small_body,
        grid=(m // tm,),
        in_specs=[pl.BlockSpec((SMALL_ROWS, k), lambda i: (0, 0)), pl.BlockSpec((tm, k), lambda i: (i, 0))],
        out_specs=pl.BlockSpec((SMALL_ROWS, tm), lambda i: (0, i)),
        out_shape=jax.ShapeDtypeStruct((SMALL_ROWS, m), F32),
        compiler_params=_cparams(("parallel",)),
        name="proj_small",
    )(w_t, x2d)


def _prep_body(s_ref, alog_ref, dtb_ref, fb_ref, beta_ref, gc_ref, cumf_ref, carry_ref, *, pad):
    j = pl.program_id(1)

    @pl.when(j == 0)
    def _():
        carry_ref[...] = jnp.zeros_like(carry_ref)

    sm = s_ref[...]
    b, a, f = sm[0:8], sm[8:16], sm[16:24]
    pos = j * LANES + lax.broadcasted_iota(I32, (8, LANES), 1)
    valid = pos >= pad
    beta = jnp.where(valid, _sigmoid(b), 0.0)
    g = jnp.where(valid, -jnp.exp(alog_ref[...]) * _softplus(a + dtb_ref[...]), 0.0)
    logf = jnp.where(valid, -_softplus(-(f + fb_ref[...])), 0.0)
    r = lax.broadcasted_iota(I32, (LANES, LANES), 0)
    c = lax.broadcasted_iota(I32, (LANES, LANES), 1)
    upper = r <= c
    shift = CHUNK.bit_length() - 1
    same_chunk = lax.shift_right_logical(r, shift) == lax.shift_right_logical(c, shift)
    m_seq = jnp.where(upper, 1.0, 0.0).astype(BF16)
    m_chunk = jnp.where(upper & same_chunk, 1.0, 0.0).astype(BF16)
    beta_ref[0] = beta
    gc_ref[0] = _dot_01(g, m_chunk)
    cum = _dot_01(logf, m_seq) + carry_ref[...]
    cumf_ref[0] = cum
    carry_ref[...] = jnp.broadcast_to(cum[:, LANES - 1:LANES], (8, LANES))


def _prep(small_t, alog, dtb, fb, batch, lp, pad):
    nb = lp // LANES
    out = jax.ShapeDtypeStruct((batch, 8, lp), F32)
    pspec = pl.BlockSpec((8, LANES), lambda b, j: (0, 0))
    ospec = pl.BlockSpec((1, 8, LANES), lambda b, j: (b, 0, j))
    return pl.pallas_call(
        functools.partial(_prep_body, pad=pad),
        grid=(batch, nb),
        in_specs=[pl.BlockSpec((SMALL_ROWS, LANES), lambda b, j: (0, b * nb + j)), pspec, pspec, pspec],
        out_specs=[ospec, ospec, ospec],
        out_shape=[out, out, out],
        scratch_shapes=[pltpu.VMEM((8, LANES), F32)],
        compiler_params=_cparams(("parallel", "arbitrary")),
        name="gate_prep",
    )(small_t, alog, dtb, fb)


SUB = 16


def _bmm(a, b):
    return jnp.einsum('cij,cjk->cik', a.astype(BF16), b.astype(BF16), preferred_element_type=F32)


def _bmm_nt(a, b):
    return jnp.einsum('cik,cjk->cij', a.astype(BF16), b.astype(BF16), preferred_element_type=F32)


def _wy_body(x_ref, halo_ref, cw_ref, beta_ref, gc_ref, u_ref, w_ref, qd_ref, kt_ref, at_ref, *, heads, cb):
    rows = cb * CHUNK
    aqk = heads * LANES
    xw = jnp.concatenate([halo_ref[0], x_ref[0]], axis=0)
    cw = cw_ref[...]
    y = cw[0:1] * xw[5:5 + rows]
    for i in range(1, CONV_K):
        y = y + cw[i:i + 1] * xw[5 + i:5 + i + rows]
    y = y * _sigmoid(y)
    ii = lax.broadcasted_iota(I32, (CHUNK, CHUNK), 0)
    jj = lax.broadcasted_iota(I32, (CHUNK, CHUNK), 1)
    eye = ii == jj
    incl = ii >= jj
    sub_shift = SUB.bit_length() - 1
    same_sub = lax.shift_right_logical(ii, sub_shift) == lax.shift_right_logical(jj, sub_shift)
    eye_f = jnp.where(eye, 1.0, 0.0)
    for h in range(heads):
        hs = slice(h * LANES, (h + 1) * LANES)
        q = y[:, h * LANES:(h + 1) * LANES]
        k = y[:, aqk + h * LANES:aqk + (h + 1) * LANES]
        v = y[:, 2 * aqk + h * LANES:2 * aqk + (h + 1) * LANES]
        q = q * lax.rsqrt(jnp.sum(q * q, axis=-1, keepdims=True) + EPS) * (LANES ** -0.5)
        k = k * lax.rsqrt(jnp.sum(k * k, axis=-1, keepdims=True) + EPS)
        q, k, v = (t.reshape(cb, CHUNK, LANES) for t in (q, k, v))
        b_row = beta_ref[0, h]
        g_row = gc_ref[0, h]
        b_col = jnp.sum(jnp.where(eye, b_row, 0.0), axis=2, keepdims=True)
        g_col = jnp.sum(jnp.where(eye, g_row, 0.0), axis=2, keepdims=True)
        gamma = jnp.where(incl, jnp.exp(jnp.where(incl, g_col - g_row, 0.0)), 0.0)
        kb = k * b_col
        a_mat = jnp.where(ii > jj, _bmm_nt(kb, k) * gamma, 0.0)
        x = jnp.where(same_sub, -a_mat, 0.0)
        t_d = eye_f + x
        for _ in range(3):
            x = _bmm(x, x)
            t_d = t_d + _bmm(t_d, x)
        m1 = _bmm(t_d, jnp.where(same_sub, 0.0, a_mat))
        m2 = _bmm(m1, m1)
        t = _bmm(eye_f - m1 + m2 - _bmm(m1, m2), t_d)
        e_g = jnp.exp(g_col)
        sol = _bmm(t, jnp.concatenate([v * b_col, kb * e_g], axis=2))
        g_last = g_row[:, :, CHUNK - 1:CHUNK]
        attn = _bmm_nt(q, k) * gamma
        u_ref[0, :, hs] = sol[:, :, :LANES].reshape(rows, LANES)
        w_ref[0, :, hs] = sol[:, :, LANES:].reshape(rows, LANES).astype(BF16)
        qd_ref[0, :, hs] = (q * e_g).reshape(rows, LANES).astype(BF16)
        kt_ref[0, :, hs] = (k * jnp.exp(g_last - g_col)).reshape(rows, LANES).astype(BF16)
        at_ref[0, :, hs] = jnp.concatenate([attn, jnp.zeros_like(attn)], axis=2).reshape(rows, LANES).astype(BF16)


def _wy(qkv_a, conv_w, beta5, gc5, batch, lp, heads):
    n_chunks = lp // CHUNK
    cb = max(c for c in range(1, 7) if n_chunks % c == 0)
    rows = cb * CHUNK
    width = heads * LANES
    ospec = pl.BlockSpec((1, rows, width), lambda b, i: (b, i, 0))
    rspec = pl.BlockSpec((1, 8, cb, 1, CHUNK), lambda b, i: (b, 0, i, 0, 0))
    f32o = jax.ShapeDtypeStruct((batch, lp, width), F32)
    b16o = jax.ShapeDtypeStruct((batch, lp, width), BF16)
    return pl.pallas_call(
        functools.partial(_wy_body, heads=heads, cb=cb),
        grid=(batch, n_chunks // cb),
        in_specs=[pl.BlockSpec((1, rows, 3 * width), lambda b, i: (b, i, 0)),
                  pl.BlockSpec((1, 8, 3 * width), lambda b, i: (b, jnp.maximum(i * (rows // 8) - 1, 0), 0)),
                  pl.BlockSpec((CONV_K, 3 * width), lambda b, i: (0, 0)), rspec, rspec],
        out_specs=[ospec] * 5,
        out_shape=[f32o, b16o, b16o, b16o, b16o],
        compiler_params=_cparams(("parallel", "parallel")),
        name="delta_rule_wy",
    )(qkv_a, qkv_a, conv_w, beta5, gc5)


def _scan_body(u_ref, w_ref, qd_ref, kt_ref, at_ref, gc_ref, zs_ref, onw_ref, o_ref, s_s, *, heads):
    @pl.when(pl.program_id(1) == 0)
    def _():
        s_s[...] = jnp.zeros_like(s_s)

    for c in range(BLOCK // CHUNK):
        rs = slice(c * CHUNK, (c + 1) * CHUNK)
        for h in range(heads):
            hs = slice(h * LANES, (h + 1) * LANES)
            s = s_s[h]
            ws = jnp.dot(jnp.concatenate([w_ref[0, rs, hs], qd_ref[0, rs, hs]], axis=0), s.astype(BF16),
                         preferred_element_type=F32)
            vb = (u_ref[0, rs, hs] - ws[:CHUNK]).astype(BF16)
            o = ws[CHUNK:] + jnp.dot(at_ref[0, rs, h * LANES:h * LANES + CHUNK], vb, preferred_element_type=F32)
            dec = jnp.exp(gc_ref[0, h, c][:, CHUNK - 1:CHUNK])
            s_s[h] = s * dec + _tn_dot(kt_ref[0, rs, hs], vb)
            on = o * lax.rsqrt(jnp.mean(o * o, axis=-1, keepdims=True) + EPS) * onw_ref[...]
            o_ref[0, rs, hs] = (on * zs_ref[0, rs, hs].astype(F32)).astype(o_ref.dtype)


def _scan(u, w, qd, kt, at, gc5, zs, onw, batch, lp, heads):
    seq = lp - BLOCK
    width = heads * LANES
    ispec = pl.BlockSpec((1, BLOCK, width), lambda b, i: (b, i, 0))
    xspec = pl.BlockSpec((1, BLOCK, width), lambda b, i: (b, jnp.maximum(i - 1, 0), 0))
    return pl.pallas_call(
        functools.partial(_scan_body, heads=heads),
        grid=(batch, lp // BLOCK),
        in_specs=[ispec] * 5 + [pl.BlockSpec((1, 8, BLOCK // CHUNK, 1, CHUNK), lambda b, i: (b, 0, i, 0, 0)),
                                xspec, pl.BlockSpec((1, LANES), lambda b, i: (0, 0))],
        out_specs=xspec,
        out_shape=jax.ShapeDtypeStruct((batch, seq, width), BF16),
        scratch_shapes=[pltpu.VMEM((heads, LANES, LANES), F32)],
        compiler_params=_cparams(("parallel", "arbitrary")),
        name="delta_rule_scan",
    )(u, w, qd, kt, at, gc5, zs, onw)


def _fox_body(q_ref, k_ref, v_ref, cum_ref, o_ref, *, tq, pad):
    i = pl.program_id(2)
    q0 = pl.multiple_of(BLOCK + i * tq, BLOCK)
    q = q_ref[0, pl.ds(q0, tq), :]
    ii = lax.broadcasted_iota(I32, (tq, tq), 0)
    jj = lax.broadcasted_iota(I32, (tq, tq), 1)
    cq = jnp.sum(jnp.where(ii == jj, cum_ref[0, :, pl.ds(q0, tq)], 0.0), axis=1, keepdims=True)

    def tile(k0, size, mask, carry):
        m, l, acc = carry
        s = _nt_dot(q, k_ref[0, pl.ds(k0, size), :]) + (cq - cum_ref[0, :, pl.ds(k0, size)])
        if mask is not None:
            s = jnp.where(mask, s, NEG_INF)
        m_new = jnp.maximum(m, jnp.max(s, axis=1, keepdims=True))
        alpha = jnp.exp(m - m_new)
        p = jnp.exp(s - m_new)
        l = alpha * l + jnp.sum(p, axis=1, keepdims=True)
        acc = alpha * acc + jnp.dot(p.astype(BF16), v_ref[0, pl.ds(k0, size), :], preferred_element_type=F32)
        return m_new, l, acc

    carry = (jnp.full((tq, 1), NEG_INF, F32), jnp.zeros((tq, 1), F32), jnp.zeros((tq, LANES), F32))
    carry = tile(0, BLOCK, lax.broadcasted_iota(I32, (tq, BLOCK), 1) >= pad, carry)
    carry = lax.fori_loop(0, i, lambda j, c: tile(pl.multiple_of(BLOCK + j * tq, BLOCK), tq, None, c), carry)
    m, l, acc = tile(q0, tq, jj <= ii, carry)
    o_ref[0] = (acc / l).astype(o_ref.dtype)


def _fox(qkv, cumf3, batch, lp, heads, pad):
    seq = lp - BLOCK
    tq = _pick_tile(seq, 512, BLOCK)
    kvspec = lambda off: pl.BlockSpec((1, lp, LANES), lambda b, h, i: (b, 0, off + h))
    return pl.pallas_call(
        functools.partial(_fox_body, tq=tq, pad=pad),
        grid=(batch, heads, seq // tq),
        in_specs=[kvspec(0), kvspec(heads), kvspec(2 * heads),
                  pl.BlockSpec((1, 1, lp), lambda b, h, i: (b * 8 + h, 0, 0))],
        out_specs=pl.BlockSpec((1, tq, LANES), lambda b, h, i: (b, i, h)),
        out_shape=jax.ShapeDtypeStruct((batch, seq, heads * LANES), BF16),
        compiler_params=_cparams(("parallel", "parallel", "arbitrary")),
        name="forgetting_attention",
    )(qkv, qkv, qkv, cumf3)


def _merge_body(oa_ref, ob_ref, g_ref, x_ref, wa_ref, wb_ref, wo_ref, nw_ref, res_ref, xn_ref):
    d = x_ref.shape[1]
    ya = jnp.dot(oa_ref[...], wa_ref[...], preferred_element_type=F32)
    yb = jnp.dot(ob_ref[...], wb_ref[...], preferred_element_type=F32)
    g = g_ref[...].astype(F32)
    mix = g[:, :d] * ya + g[:, d:] * yb
    res = x_ref[...] + jnp.dot(mix.astype(BF16), wo_ref[...], preferred_element_type=F32)
    res_ref[...] = res
    xn = res * lax.rsqrt(jnp.mean(res * res, axis=-1, keepdims=True) + EPS) * nw_ref[...]
    xn_ref[...] = xn.astype(xn_ref.dtype)


def _merge(oa, ob, gates, x2d, wa, wb, wo, nw):
    m, d = x2d.shape
    bw = oa.shape[1]
    tm = _pick_tile(m, 512, 128)
    row = lambda c: pl.BlockSpec((tm, c), lambda i: (i, 0))
    full = lambda r, c: pl.BlockSpec((r, c), lambda i: (0, 0))
    return pl.pallas_call(
        _merge_body,
        grid=(m // tm,),
        in_specs=[row(bw), row(bw), row(2 * d), row(d), full(bw, d), full(bw, d), full(d, d), full(1, d)],
        out_specs=[row(d), row(d)],
        out_shape=[jax.ShapeDtypeStruct((m, d), F32), jax.ShapeDtypeStruct((m, d), BF16)],
        compiler_params=_cparams(("parallel",)),
        name="merge_out_proj",
    )(oa, ob, gates, x2d, wa, wb, wo, nw)


_CAND_GROUPS = ((0, 0, 8, 8), (0, 8, 16, 8), (1, 0, 8, 8), (2, 0, 8, 5), (3, 0, 8, 4), (4, 0, 8, 3),
                (5, 0, 8, 2), (6, 0, 8, 2), (7, 0, 8, 2))


def _top_rows(x, key, n, val_ref, key_ref=None, payloads=()):
    big = jnp.float32(1e9)
    for kk in range(n):
        m = jnp.max(x, axis=0, keepdims=True)
        pos = jnp.min(jnp.where(x == m, key, big), axis=0, keepdims=True)
        sel = key == pos
        val_ref[kk:kk + 1, :] = m
        if key_ref is not None:
            key_ref[kk:kk + 1, :] = pos
        for ref, pay in payloads:
            ref[kk:kk + 1, :] = jnp.sum(jnp.where(sel, pay, 0.0), axis=0, keepdims=True)
        x = jnp.where(sel, -jnp.inf, x)


def _route_body(xn_ref, wqt_ref, keys_ref, g_ref, a_ref, b_ref, qt_s, v1_s, i1_s, v2_s, i2_s,
                sc_s, ea_s, eb_s, gt_s, at_s, bt_s, *, heads, n_keys):
    tm = xn_ref.shape[0]
    qt_s[...] = _nt_dot(wqt_ref[...], xn_ref[...]).astype(BF16)
    key_iota = lax.broadcasted_iota(I32, (n_keys, LANES), 0).astype(F32)
    sub = lax.broadcasted_iota(I32, (8, LANES), 0)

    def head(h, carry):
        for half in range(tm // LANES):
            ls = slice(half * LANES, (half + 1) * LANES)
            for p, (v_s, i_s) in enumerate(((v1_s, i1_s), (v2_s, i2_s))):
                qhp = qt_s[pl.ds(pl.multiple_of(h * 2 * LANES + p * LANES, LANES), LANES), ls]
                kk = keys_ref[pl.ds(pl.multiple_of((p * heads + h) * n_keys, n_keys), n_keys), :]
                st = jnp.dot(kk, qhp, preferred_element_type=F32)
                _top_rows(st, key_iota, TOPK, v_s, i_s)
            v1, i1, v2, i2 = v1_s[...], i1_s[...], v2_s[...], i2_s[...]
            cands, flats, cas, cbs = [], [], [], []
            for (a, b0, b1, nvalid) in _CAND_GROUPS:
                ok = sub < nvalid
                cands.append(jnp.where(ok, v1[a:a + 1] + v2[b0:b1], -jnp.inf))
                flats.append((a * TOPK + b0 + sub).astype(F32))
                cas.append(jnp.broadcast_to(i1[a:a + 1], (8, LANES)))
                cbs.append(i2[b0:b1])
            cands.append(v1[8:16] + v2[0:1])
            flats.append(((8 + sub) * TOPK).astype(F32))
            cas.append(i1[8:16])
            cbs.append(jnp.broadcast_to(i2[0:1], (8, LANES)))
            cat = lambda xs: jnp.concatenate(xs, axis=0)
            _top_rows(cat(cands), cat(flats), TOPK, sc_s, payloads=((ea_s, cat(cas)), (eb_s, cat(cbs))))
            sc = sc_s[...]
            e = jnp.exp(sc - sc[0:1])
            row = pl.ds(pl.multiple_of(h * TOPK, TOPK), TOPK)
            gt_s[row, ls] = e / jnp.sum(e, axis=0, keepdims=True)
            at_s[row, ls] = ea_s[...]
            bt_s[row, ls] = eb_s[...]
        return carry

    lax.fori_loop(0, heads, head, 0)
    g_ref[...] = gt_s[...].T
    a_ref[...] = at_s[...].T
    b_ref[...] = bt_s[...].T


def _route(xn, wq_t, keys2d, heads, n_keys):
    t, d = xn.shape
    tm = _pick_tile(t, 256, LANES)
    hk = heads * TOPK
    out = jax.ShapeDtypeStruct((t, hk), F32)
    ospec = pl.BlockSpec((tm, hk), lambda i: (i, 0))
    small = lambda: pltpu.VMEM((TOPK, LANES), F32)
    return pl.pallas_call(
        functools.partial(_route_body, heads=heads, n_keys=n_keys),
        grid=(t // tm,),
        in_specs=[pl.BlockSpec((tm, d), lambda i: (i, 0)),
                  pl.BlockSpec(wq_t.shape, lambda i: (0, 0)),
                  pl.BlockSpec(keys2d.shape, lambda i: (0, 0))],
        out_specs=[ospec, ospec, ospec],
        out_shape=[out, out, out],
        scratch_shapes=[pltpu.VMEM((wq_t.shape[0], tm), BF16)] + [small() for _ in range(7)]
        + [pltpu.VMEM((hk, tm), F32) for _ in range(3)],
        compiler_params=_cparams(("parallel",)),
        name="peer_route",
    )(xn, wq_t, keys2d)


def _gelu(x):
    return 0.5 * x * (1.0 + lax.erf(x * (2.0 ** -0.5)))


W_PITCH_PAD = 8
MXU_DIM = 256


def _pack_bf16_pair(a, b):
    ua = lax.bitcast_convert_type(a, jnp.uint32) + jnp.uint32(0x8000)
    ub = lax.bitcast_convert_type(b, jnp.uint32) + jnp.uint32(0x8000)
    return (ua & jnp.uint32(0xFFFF0000)) | lax.shift_right_logical(ub, jnp.uint32(16))


def _unpack_bf16_pair(w, index):
    bits = (w & jnp.uint32(0xFFFF0000)) if index == 0 else lax.shift_left(w, jnp.uint32(16))
    return lax.bitcast_convert_type(bits, F32)


def _expert_body(xn_ref, res_ref, g_ref, a_ref, b_ref, ut_ref, v_ref, o_ref, w_s, acc_s, *, n_keys):
    j = pl.program_id(1)
    tm = xn_ref.shape[0]
    te = ut_ref.shape[1]
    pitch = n_keys + W_PITCH_PAD

    th = tm // 2

    @pl.when(j == 0)
    def _():
        acc_s[...] = jnp.zeros_like(acc_s)
        key = lax.broadcasted_iota(I32, (n_keys, g_ref.shape[1]), 0).astype(F32)

        def w_token(t):
            pt = jnp.where(key == a_ref[pl.ds(t, 1), :], g_ref[pl.ds(t, 1), :], 0.0).astype(BF16)
            qt = jnp.where(key == b_ref[pl.ds(t, 1), :], 1.0, 0.0).astype(BF16)
            return _nt_dot(pt, qt)

        def pair(t, carry):
            packed = _pack_bf16_pair(w_token(t), w_token(t + th))
            w_s[pl.ds(pl.multiple_of(t * pitch, 8), n_keys), :] = packed
            return carry

        lax.fori_loop(0, th, pair, 0, unroll=16)

    n_slab = te // n_keys
    for half in range(2):
        rows = slice(half * th, (half + 1) * th)
        act = jnp.dot(xn_ref[rows, :], ut_ref[...], preferred_element_type=F32)
        coef = []
        for s in range(n_slab):
            packed = w_s[pl.ds(j * n_slab + s, th, stride=pitch), :]
            w_slab = _unpack_bf16_pair(packed, half)
            coef.append((_gelu(act[:, s * n_keys:(s + 1) * n_keys]) * w_slab).astype(BF16))
        acc_s[rows, :] += jnp.dot(jnp.concatenate(coef, axis=1), v_ref[...], preferred_element_type=F32)

    @pl.when(j == pl.num_programs(1) - 1)
    def _():
        o_ref[...] = res_ref[...] + acc_s[...]


def _experts(xn, res, g, a, b, u_t, v, n_keys):
    t, d = xn.shape
    ne = v.shape[0]
    hk = g.shape[1]
    tm = _pick_tile(t, 512, 2 * LANES)
    te = _pick_tile(ne, 1024, MXU_DIM)
    tok = lambda c: pl.BlockSpec((tm, c), lambda i, j: (i, 0))
    return pl.pallas_call(
        functools.partial(_expert_body, n_keys=n_keys),
        grid=(t // tm, ne // te),
        in_specs=[tok(d), tok(d), tok(hk), tok(hk), tok(hk),
                  pl.BlockSpec((d, te), lambda i, j: (0, j)), pl.BlockSpec((te, d), lambda i, j: (j, 0))],
        out_specs=tok(d),
        out_shape=jax.ShapeDtypeStruct((t, d), F32),
        scratch_shapes=[pltpu.VMEM(((n_keys + W_PITCH_PAD) * (tm // 2), n_keys), jnp.uint32),
                        pltpu.VMEM((tm, d), F32)],
        compiler_params=_cparams(("parallel", "arbitrary")),
        name="peer_experts",
    )(xn, res, g, a, b, u_t, v)


def _layer(x, meta_tokens, norm_mix, w_in, conv_w, a_log, dt_bias, o_norm_a, q_norm_b, k_norm_b, f_bias,
           w_branch, w_out, norm_ffn, peer_wq, peer_sub_keys, expert_u, expert_v):
    batch, seq, d = x.shape
    n_meta = meta_tokens.shape[0]
    pad = BLOCK - n_meta
    lp = BLOCK + seq
    ha, hb = a_log.shape[0], f_bias.shape[0]
    aqk = ha * LANES
    bw = hb * LANES
    assert ha <= 8 and hb <= 8 and seq % BLOCK == 0 and conv_w.shape == (CONV_K, 3 * aqk)
    assert w_in.shape[1] == 4 * aqk + 2 * ha + 3 * bw + hb + 2 * d
    c_z, c_b, c_a = 3 * aqk, 4 * aqk, 4 * aqk + ha
    c_qkvb = 4 * aqk + 2 * ha
    c_f = c_qkvb + 3 * bw
    c_g = c_f + hb

    x2d = x.reshape(batch * seq, d)
    hn_x = _rmsnorm(x2d, norm_mix, BF16)
    hn_m = _rmsnorm(meta_tokens, norm_mix, BF16)
    hn_p = jnp.concatenate([jnp.zeros((batch, pad, d), BF16), jnp.broadcast_to(hn_m[None], (batch, n_meta, d)),
                            hn_x.reshape(batch, seq, d)], axis=1).reshape(batch * lp, d)

    wb16 = w_in.astype(BF16)
    qkv_a = _proj(_proj_plain_body, hn_p, wb16[:, :c_z], F32, _pick_tile(c_z, 1024, LANES), name="proj_qkv_a")
    zs = _proj(_proj_silu_body, hn_x, wb16[:, c_z:c_b], BF16, _pick_tile(aqk, 1024, LANES), name="proj_z")
    scale = LANES ** -0.5
    nw = jnp.concatenate([jnp.tile(q_norm_b * scale, hb), jnp.tile(k_norm_b, hb), jnp.ones((bw,), F32)]).reshape(1, 3 * bw)
    qkv_b = _proj(_proj_qkvb_body, hn_p, wb16[:, c_qkvb:c_f], BF16, bw, extra=(nw,), name="proj_qkv_b")
    gates = _proj(_proj_sigmoid_body, hn_x, wb16[:, c_g:], BF16, _pick_tile(2 * d, 1024, LANES), name="proj_gates")

    def rows8(w):
        return jnp.pad(w.T, ((0, 8 - w.shape[1]), (0, 0)))

    w_small = jnp.concatenate([rows8(wb16[:, c_b:c_a]), rows8(wb16[:, c_a:c_qkvb]), rows8(wb16[:, c_f:c_g]),
                               jnp.zeros((SMALL_ROWS - 24, d), BF16)], axis=0)
    small_t = _proj_small_t(hn_p, w_small)

    def lanes8(p):
        return jnp.broadcast_to(jnp.pad(p.astype(F32), (0, 8 - p.shape[0]))[:, None], (8, LANES))

    beta, gc, cumf = _prep(small_t, lanes8(a_log), lanes8(dt_bias), lanes8(f_bias), batch, lp, pad)
    n_chunks = lp // CHUNK
    beta5 = beta.reshape(batch, 8, n_chunks, 1, CHUNK)
    gc5 = gc.reshape(batch, 8, n_chunks, 1, CHUNK)
    u, w, qd, kt, at = _wy(qkv_a.reshape(batch, lp, 3 * aqk), conv_w, beta5, gc5, batch, lp, ha)
    o_a = _scan(u, w, qd, kt, at, gc5, zs.reshape(batch, seq, aqk), o_norm_a.reshape(1, LANES).astype(F32),
                batch, lp, ha)

    o_b = _fox(qkv_b.reshape(batch, lp, 3 * bw), cumf.reshape(batch * 8, 1, lp), batch, lp, hb, pad)

    res, xn = _merge(o_a.reshape(batch * seq, aqk), o_b.reshape(batch * seq, bw), gates, x2d,
                     w_branch[0].astype(BF16), w_branch[1].astype(BF16), w_out.astype(BF16),
                     norm_ffn.reshape(1, d).astype(F32))

    hp, n_keys = peer_sub_keys.shape[1], peer_sub_keys.shape[2]
    assert peer_sub_keys.shape[3] == LANES and n_keys == LANES and peer_wq.shape[1] == hp * 2 * LANES
    g_w, a_k, b_k = _route(xn, peer_wq.T.astype(BF16), peer_sub_keys.reshape(2 * hp * n_keys, LANES).astype(BF16),
                           hp, n_keys)
    out = _experts(xn, res, g_w, a_k, b_k, expert_u.T.astype(BF16), expert_v.astype(BF16), n_keys)
    return out.reshape(batch, seq, d)


def kernel(x, meta_tokens, norm_mix, w_in, conv_w, a_log, dt_bias, o_norm_a, q_norm_b, k_norm_b, f_bias,
           w_branch, w_out, norm_ffn, peer_wq, peer_sub_keys, expert_u, expert_v):
    assert norm_mix.shape[0] == 1, "single-layer block"
    return _layer(x, meta_tokens, norm_mix[0], w_in[0], conv_w[0], a_log[0], dt_bias[0], o_norm_a[0],
                  q_norm_b[0], k_norm_b[0], f_bias[0], w_branch[0], w_out[0], norm_ffn[0], peer_wq[0],
                  peer_sub_keys[0], expert_u[0], expert_v[0])
```

```python
import functools
import math

import jax
import jax.numpy as jnp
from jax import lax
from jax.experimental import pallas as pl
from jax.experimental.pallas import tpu as pltpu

F32, BF16, I32 = jnp.float32, jnp.bfloat16, jnp.int32
EPS = 1e-6
NEG_INF = -1e30
LANES = 128
BLOCK = 128
CHUNK = 64
CONV_K = 4
TOPK = 16
SMALL_ROWS = 32
VMEM_LIMIT = 56 * 1024 * 1024


def _cparams(sem):
    return pltpu.CompilerParams(dimension_semantics=sem, vmem_limit_bytes=VMEM_LIMIT)


def _pick_tile(n, target, mult):
    best = None
    for t in range(mult, min(n, target) + 1, mult):
        if n % t == 0:
            best = t
    assert best is not None, (n, target, mult)
    return best


def _nt_dot(a, b):
    return lax.dot_general(a, b, (((1,), (1,)), ((), ())), preferred_element_type=F32)


def _tn_dot(a, b):
    return lax.dot_general(a, b, (((0,), (0,)), ((), ())), preferred_element_type=F32)


def _split3(x):
    hi = x.astype(BF16)
    r = x - hi.astype(F32)
    mid = r.astype(BF16)
    lo = (r - mid.astype(F32)).astype(BF16)
    return hi, mid, lo


def _dot_01(x, m01):
    hi, mid, lo = _split3(x)
    d = functools.partial(jnp.dot, preferred_element_type=F32)
    return d(hi, m01) + (d(mid, m01) + d(lo, m01))


def _sigmoid(x):
    return 1.0 / (1.0 + jnp.exp(-x))


def _softplus(x):
    return jnp.maximum(x, 0.0) + jnp.log1p(jnp.exp(-jnp.abs(x)))


def _rmsnorm_body(x_ref, w_ref, o_ref):
    x = x_ref[...].astype(F32)
    y = x * lax.rsqrt(jnp.mean(x * x, axis=-1, keepdims=True) + EPS)
    o_ref[...] = (y * w_ref[...]).astype(o_ref.dtype)


def _rmsnorm(x2d, w, out_dtype):
    m, d = x2d.shape
    tm = _pick_tile(m, 1024, 8)
    return pl.pallas_call(
        _rmsnorm_body,
        grid=(m // tm,),
        in_specs=[pl.BlockSpec((tm, d), lambda i: (i, 0)), pl.BlockSpec((1, d), lambda i: (0, 0))],
        out_specs=pl.BlockSpec((tm, d), lambda i: (i, 0)),
        out_shape=jax.ShapeDtypeStruct((m, d), out_dtype),
        compiler_params=_cparams(("parallel",)),
        name="rmsnorm",
    )(x2d, w.reshape(1, d).astype(F32))


def _proj_plain_body(x_ref, w_ref, o_ref):
    o_ref[...] = jnp.dot(x_ref[...], w_ref[...], preferred_element_type=F32).astype(o_ref.dtype)


def _proj_silu_body(x_ref, w_ref, o_ref):
    y = jnp.dot(x_ref[...], w_ref[...], preferred_element_type=F32)
    o_ref[...] = (y * _sigmoid(y)).astype(o_ref.dtype)


def _proj_sigmoid_body(x_ref, w_ref, o_ref):
    y = jnp.dot(x_ref[...], w_ref[...], preferred_element_type=F32)
    o_ref[...] = _sigmoid(y).astype(o_ref.dtype)


def _proj_qkvb_body(x_ref, w_ref, nw_ref, o_ref):
    y = jnp.dot(x_ref[...], w_ref[...], preferred_element_type=F32)
    do_norm = pl.program_id(1) < 2
    for h in range(y.shape[1] // LANES):
        sl = slice(h * LANES, (h + 1) * LANES)
        seg = y[:, sl]
        r = lax.rsqrt(jnp.mean(seg * seg, axis=-1, keepdims=True) + EPS)
        r = jnp.where(do_norm, r, 1.0)
        o_ref[:, sl] = (seg * r * nw_ref[:, sl]).astype(o_ref.dtype)


def _proj(body, x2d, w, out_dtype, tn, extra=(), name="proj"):
    m, k = x2d.shape
    n = w.shape[1]
    tm = _pick_tile(m, 1536, 128)
    assert n % tn == 0
    in_specs = [pl.BlockSpec((tm, k), lambda i, j: (i, 0)), pl.BlockSpec((k, tn), lambda i, j: (0, j))]
    in_specs += [pl.BlockSpec((1, tn), lambda i, j: (0, j)) for _ in extra]
    return pl.pallas_call(
        body,
        grid=(m // tm, n // tn),
        in_specs=in_specs,
        out_specs=pl.BlockSpec((tm, tn), lambda i, j: (i, j)),
        out_shape=jax.ShapeDtypeStruct((m, n), out_dtype),
        compiler_params=_cparams(("parallel", "arbitrary")),
        name=name,
    )(x2d, w, *extra)


def _small_body(w_ref, x_ref, o_ref):
    o_ref[...] = _nt_dot(w_ref[...], x_ref[...])


def _proj_small_t(x2d, w_t):
    m, k = x2d.shape
    tm = _pick_tile(m, 2048, 128)
    return pl.pallas_call(
        _small_body,
        grid=(m // tm,),
        in_specs=[pl.BlockSpec((SMALL_ROWS, k), lambda i: (0, 0)), pl.BlockSpec((tm, k), lambda i: (i, 0))],
        out_specs=pl.BlockSpec((SMALL_ROWS, tm), lambda i: (0, i)),
        out_shape=jax.ShapeDtypeStruct((SMALL_ROWS, m), F32),
        compiler_params=_cparams(("parallel",)),
        name="proj_small",
    )(w_t, x2d)


def _prep_body(s_ref, alog_ref, dtb_ref, fb_ref, beta_ref, gc_ref, cumf_ref, carry_ref, *, pad):
    j = pl.program_id(1)

    @pl.when(j == 0)
    def _():
        carry_ref[...] = jnp.zeros_like(carry_ref)

    sm = s_ref[...]
    b, a, f = sm[0:8], sm[8:16], sm[16:24]
    pos = j * LANES + lax.broadcasted_iota(I32, (8, LANES), 1)
    valid = pos >= pad
    beta = jnp.where(valid, _sigmoid(b), 0.0)
    g = jnp.where(valid, -jnp.exp(alog_ref[...]) * _softplus(a + dtb_ref[...]), 0.0)
    logf = jnp.where(valid, -_softplus(-(f + fb_ref[...])), 0.0)
    r = lax.broadcasted_iota(I32, (LANES, LANES), 0)
    c = lax.broadcasted_iota(I32, (LANES, LANES), 1)
    upper = r <= c
    shift = CHUNK.bit_length() - 1
    same_chunk = lax.shift_right_logical(r, shift) == lax.shift_right_logical(c, shift)
    m_seq = jnp.where(upper, 1.0, 0.0).astype(BF16)
    m_chunk = jnp.where(upper & same_chunk, 1.0, 0.0).astype(BF16)
    beta_ref[0] = beta
    gc_ref[0] = _dot_01(g, m_chunk)
    cum = _dot_01(logf, m_seq) + carry_ref[...]
    cumf_ref[0] = cum
    carry_ref[...] = jnp.broadcast_to(cum[:, LANES - 1:LANES], (8, LANES))


def _prep(small_t, alog, dtb, fb, batch, lp, pad):
    nb = lp // LANES
    out = jax.ShapeDtypeStruct((batch, 8, lp), F32)
    pspec = pl.BlockSpec((8, LANES), lambda b, j: (0, 0))
    ospec = pl.BlockSpec((1, 8, LANES), lambda b, j: (b, 0, j))
    return pl.pallas_call(
        functools.partial(_prep_body, pad=pad),
        grid=(batch, nb),
        in_specs=[pl.BlockSpec((SMALL_ROWS, LANES), lambda b, j: (0, b * nb + j)), pspec, pspec, pspec],
        out_specs=[ospec, ospec, ospec],
        out_shape=[out, out, out],
        scratch_shapes=[pltpu.VMEM((8, LANES), F32)],
        compiler_params=_cparams(("parallel", "arbitrary")),
        name="gate_prep",
    )(small_t, alog, dtb, fb)


SUB = 16


def _bmm(a, b):
    return jnp.einsum('cij,cjk->cik', a.astype(BF16), b.astype(BF16), preferred_element_type=F32)


def _bmm_nt(a, b):
    return jnp.einsum('cik,cjk->cij', a.astype(BF16), b.astype(BF16), preferred_element_type=F32)


def _wy_body(x_ref, halo_ref, cw_ref, beta_ref, gc_ref, u_ref, w_ref, qd_ref, kt_ref, at_ref, *, heads, cb):
    rows = cb * CHUNK
    aqk = heads * LANES
    xw = jnp.concatenate([halo_ref[0], x_ref[0]], axis=0)
    cw = cw_ref[...]
    y = cw[0:1] * xw[5:5 + rows]
    for i in range(1, CONV_K):
        y = y + cw[i:i + 1] * xw[5 + i:5 + i + rows]
    y = y * _sigmoid(y)
    ii = lax.broadcasted_iota(I32, (CHUNK, CHUNK), 0)
    jj = lax.broadcasted_iota(I32, (CHUNK, CHUNK), 1)
    eye = ii == jj
    incl = ii >= jj
    sub_shift = SUB.bit_length() - 1
    same_sub = lax.shift_right_logical(ii, sub_shift) == lax.shift_right_logical(jj, sub_shift)
    eye_f = jnp.where(eye, 1.0, 0.0)

    def heads_batched(off):
        return jnp.concatenate([y[:, off + h * LANES:off + (h + 1) * LANES].reshape(cb, CHUNK, LANES)
                                for h in range(heads)], axis=0)

    q, k, v = heads_batched(0), heads_batched(aqk), heads_batched(2 * aqk)
    q = q * lax.rsqrt(jnp.sum(q * q, axis=-1, keepdims=True) + EPS) * (LANES ** -0.5)
    k = k * lax.rsqrt(jnp.sum(k * k, axis=-1, keepdims=True) + EPS)
    b_row = jnp.concatenate([beta_ref[0, h] for h in range(heads)], axis=0)
    g_row = jnp.concatenate([gc_ref[0, h] for h in range(heads)], axis=0)
    b_col = jnp.sum(jnp.where(eye, b_row, 0.0), axis=2, keepdims=True)
    g_col = jnp.sum(jnp.where(eye, g_row, 0.0), axis=2, keepdims=True)
    gamma = jnp.where(incl, jnp.exp(jnp.where(incl, g_col - g_row, 0.0)), 0.0)
    kb = k * b_col
    a_mat = jnp.where(ii > jj, _bmm_nt(kb, k) * gamma, 0.0)
    attn = _bmm_nt(q, k) * gamma
    x = jnp.where(same_sub, -a_mat, 0.0)
    t_d = eye_f + x
    for _ in range(3):
        x = _bmm(x, x)
        t_d = t_d + _bmm(t_d, x)
    m1 = _bmm(t_d, jnp.where(same_sub, 0.0, a_mat))
    m2 = _bmm(m1, m1)
    t = _bmm(eye_f - m1 + m2 - _bmm(m1, m2), t_d)
    e_g = jnp.exp(g_col)
    sol = _bmm(t, jnp.concatenate([v * b_col, kb * e_g], axis=2))
    g_last = g_row[:, :, CHUNK - 1:CHUNK]
    qd = q * e_g
    kt = k * jnp.exp(g_last - g_col)
    at = jnp.concatenate([attn, jnp.zeros_like(attn)], axis=2)
    for h in range(heads):
        hs = slice(h * LANES, (h + 1) * LANES)
        bs = slice(h * cb, (h + 1) * cb)
        u_ref[0, :, hs] = sol[bs, :, :LANES].reshape(rows, LANES)
        w_ref[0, :, hs] = sol[bs, :, LANES:].reshape(rows, LANES).astype(BF16)
        qd_ref[0, :, hs] = qd[bs].reshape(rows, LANES).astype(BF16)
        kt_ref[0, :, hs] = kt[bs].reshape(rows, LANES).astype(BF16)
        at_ref[0, :, hs] = at[bs].reshape(rows, LANES).astype(BF16)


def _wy(qkv_a, conv_w, beta5, gc5, batch, lp, heads):
    n_chunks = lp // CHUNK
    cb = max(c for c in range(1, 7) if n_chunks % c == 0)
    rows = cb * CHUNK
    width = heads * LANES
    ospec = pl.BlockSpec((1, rows, width), lambda b, i: (b, i, 0))
    rspec = pl.BlockSpec((1, 8, cb, 1, CHUNK), lambda b, i: (b, 0, i, 0, 0))
    f32o = jax.ShapeDtypeStruct((batch, lp, width), F32)
    b16o = jax.ShapeDtypeStruct((batch, lp, width), BF16)
    return pl.pallas_call(
        functools.partial(_wy_body, heads=heads, cb=cb),
        grid=(batch, n_chunks // cb),
        in_specs=[pl.BlockSpec((1, rows, 3 * width), lambda b, i: (b, i, 0)),
                  pl.BlockSpec((1, 8, 3 * width), lambda b, i: (b, jnp.maximum(i * (rows // 8) - 1, 0), 0)),
                  pl.BlockSpec((CONV_K, 3 * width), lambda b, i: (0, 0)), rspec, rspec],
        out_specs=[ospec] * 5,
        out_shape=[f32o, b16o, b16o, b16o, b16o],
        compiler_params=_cparams(("parallel", "parallel")),
        name="delta_rule_wy",
    )(qkv_a, qkv_a, conv_w, beta5, gc5)


def _scan_body(u_ref, w_ref, qd_ref, kt_ref, at_ref, gc_ref, zs_ref, onw_ref, o_ref, s_s, *, heads):
    @pl.when(pl.program_id(1) == 0)
    def _():
        s_s[...] = jnp.zeros_like(s_s)

    hsl = [slice(h * LANES, (h + 1) * LANES) for h in range(heads)]
    for c in range(BLOCK // CHUNK):
        rs = slice(c * CHUNK, (c + 1) * CHUNK)
        s = [s_s[h] for h in range(heads)]
        ws = [jnp.dot(jnp.concatenate([w_ref[0, rs, hs], qd_ref[0, rs, hs]], axis=0), s[h].astype(BF16),
                      preferred_element_type=F32) for h, hs in enumerate(hsl)]
        vb = [(u_ref[0, rs, hs] - ws[h][:CHUNK]).astype(BF16) for h, hs in enumerate(hsl)]
        upd = [_tn_dot(kt_ref[0, rs, hs], vb[h]) for h, hs in enumerate(hsl)]
        o = [ws[h][CHUNK:] + jnp.dot(at_ref[0, rs, h * LANES:h * LANES + CHUNK], vb[h], preferred_element_type=F32)
             for h in range(heads)]
        for h, hs in enumerate(hsl):
            dec = jnp.exp(gc_ref[0, h, c][:, CHUNK - 1:CHUNK])
            s_s[h] = s[h] * dec + upd[h]
            on = o[h] * lax.rsqrt(jnp.mean(o[h] * o[h], axis=-1, keepdims=True) + EPS) * onw_ref[...]
            o_ref[0, rs, hs] = (on * zs_ref[0, rs, hs].astype(F32)).astype(o_ref.dtype)


def _scan(u, w, qd, kt, at, gc5, zs, onw, batch, lp, heads):
    seq = lp - BLOCK
    width = heads * LANES
    ispec = pl.BlockSpec((1, BLOCK, width), lambda b, i: (b, i, 0))
    xspec = pl.BlockSpec((1, BLOCK, width), lambda b, i: (b, jnp.maximum(i - 1, 0), 0))
    return pl.pallas_call(
        functools.partial(_scan_body, heads=heads),
        grid=(batch, lp // BLOCK),
        in_specs=[ispec] * 5 + [pl.BlockSpec((1, 8, BLOCK // CHUNK, 1, CHUNK), lambda b, i: (b, 0, i, 0, 0)),
                                xspec, pl.BlockSpec((1, LANES), lambda b, i: (0, 0))],
        out_specs=xspec,
        out_shape=jax.ShapeDtypeStruct((batch, seq, width), BF16),
        scratch_shapes=[pltpu.VMEM((heads, LANES, LANES), F32)],
        compiler_params=_cparams(("parallel", "arbitrary")),
        name="delta_rule_scan",
    )(u, w, qd, kt, at, gc5, zs, onw)


def _fox_body(q_ref, k_ref, v_ref, cum_ref, o_ref, *, tq, pad):
    i = pl.program_id(2)
    q0 = pl.multiple_of(BLOCK + i * tq, BLOCK)
    q = q_ref[0, pl.ds(q0, tq), :]

    def tile(k0, size, mask, carry):
        m, l, acc = carry
        s = _nt_dot(q, k_ref[0, pl.ds(k0, size), :]) - cum_ref[0, :, pl.ds(k0, size)]
        if mask is not None:
            s = jnp.where(mask, s, NEG_INF)
        m_new = jnp.maximum(m, jnp.max(s, axis=1, keepdims=True))
        alpha = jnp.exp(m - m_new)
        p = jnp.exp(s - m_new)
        l = alpha * l + jnp.sum(p, axis=1, keepdims=True)
        acc = alpha * acc + jnp.dot(p.astype(BF16), v_ref[0, pl.ds(k0, size), :], preferred_element_type=F32)
        return m_new, l, acc

    carry = (jnp.full((tq, 1), NEG_INF, F32), jnp.zeros((tq, 1), F32), jnp.zeros((tq, LANES), F32))
    carry = tile(0, BLOCK, lax.broadcasted_iota(I32, (tq, BLOCK), 1) >= pad, carry)
    carry = lax.fori_loop(0, i, lambda j, c: tile(pl.multiple_of(BLOCK + j * tq, BLOCK), tq, None, c), carry)
    causal = lax.broadcasted_iota(I32, (tq, tq), 1) <= lax.broadcasted_iota(I32, (tq, tq), 0)
    m, l, acc = tile(q0, tq, causal, carry)
    o_ref[0] = (acc / l).astype(o_ref.dtype)


def _fox(qkv, cumf3, batch, lp, heads, pad):
    seq = lp - BLOCK
    tq = _pick_tile(seq, 512, 2 * BLOCK)
    kvspec = lambda off: pl.BlockSpec((1, lp, LANES), lambda b, h, i: (b, 0, off + h))
    return pl.pallas_call(
        functools.partial(_fox_body, tq=tq, pad=pad),
        grid=(batch, heads, seq // tq),
        in_specs=[kvspec(0), kvspec(heads), kvspec(2 * heads),
                  pl.BlockSpec((1, 1, lp), lambda b, h, i: (b * 8 + h, 0, 0))],
        out_specs=pl.BlockSpec((1, tq, LANES), lambda b, h, i: (b, i, h)),
        out_shape=jax.ShapeDtypeStruct((batch, seq, heads * LANES), BF16),
        compiler_params=_cparams(("parallel", "parallel", "arbitrary")),
        name="forgetting_attention",
    )(qkv, qkv, qkv, cumf3)


def _merge_body(oa_ref, ob_ref, g_ref, x_ref, wa_ref, wb_ref, wo_ref, nw_ref, res_ref, xn_ref):
    d = x_ref.shape[1]
    ya = jnp.dot(oa_ref[...], wa_ref[...], preferred_element_type=F32)
    yb = jnp.dot(ob_ref[...], wb_ref[...], preferred_element_type=F32)
    g = g_ref[...].astype(F32)
    mix = g[:, :d] * ya + g[:, d:] * yb
    res = x_ref[...] + jnp.dot(mix.astype(BF16), wo_ref[...], preferred_element_type=F32)
    res_ref[...] = res
    xn = res * lax.rsqrt(jnp.mean(res * res, axis=-1, keepdims=True) + EPS) * nw_ref[...]
    xn_ref[...] = xn.astype(xn_ref.dtype)


def _merge(oa, ob, gates, x2d, wa, wb, wo, nw):
    m, d = x2d.shape
    bw = oa.shape[1]
    tm = _pick_tile(m, 512, 128)
    row = lambda c: pl.BlockSpec((tm, c), lambda i: (i, 0))
    full = lambda r, c: pl.BlockSpec((r, c), lambda i: (0, 0))
    return pl.pallas_call(
        _merge_body,
        grid=(m // tm,),
        in_specs=[row(bw), row(bw), row(2 * d), row(d), full(bw, d), full(bw, d), full(d, d), full(1, d)],
        out_specs=[row(d), row(d)],
        out_shape=[jax.ShapeDtypeStruct((m, d), F32), jax.ShapeDtypeStruct((m, d), BF16)],
        compiler_params=_cparams(("parallel",)),
        name="merge_out_proj",
    )(oa, ob, gates, x2d, wa, wb, wo, nw)


_CAND_GROUPS = ((0, 0, 8, 8), (0, 8, 16, 8), (1, 0, 8, 8), (2, 0, 8, 5), (3, 0, 8, 4), (4, 0, 8, 3),
                (5, 0, 8, 2), (6, 0, 8, 2), (7, 0, 8, 2))


def _top_rows(x, key, n, val_ref, key_ref=None, payloads=()):
    big = jnp.float32(1e9)
    for kk in range(n):
        m = jnp.max(x, axis=0, keepdims=True)
        pos = jnp.min(jnp.where(x == m, key, big), axis=0, keepdims=True)
        sel = key == pos
        val_ref[kk:kk + 1, :] = m
        if key_ref is not None:
            key_ref[kk:kk + 1, :] = pos
        for ref, pay in payloads:
            ref[kk:kk + 1, :] = jnp.sum(jnp.where(sel, pay, 0.0), axis=0, keepdims=True)
        x = jnp.where(sel, -jnp.inf, x)


def _route_body(xn_ref, wqt_ref, keys_ref, g_ref, a_ref, b_ref, qt_s, gt_s, at_s, bt_s, v_s, i_s, sc_s, ea_s, eb_s,
                *, heads, n_keys):
    tm = xn_ref.shape[0]
    qt_s[...] = _nt_dot(wqt_ref[...], xn_ref[...]).astype(BF16)
    key_iota = lax.broadcasted_iota(I32, (n_keys, 2 * tm), 0).astype(F32)
    sub = lax.broadcasted_iota(I32, (8, tm), 0)

    def head(h, carry):
        st = []
        for p in range(2):
            qhp = qt_s[pl.ds(pl.multiple_of(h * 2 * LANES + p * LANES, LANES), LANES), :]
            kk = keys_ref[pl.ds(pl.multiple_of((p * heads + h) * n_keys, n_keys), n_keys), :]
            st.append(jnp.dot(kk, qhp, preferred_element_type=F32))
        _top_rows(jnp.concatenate(st, axis=1), key_iota, TOPK, v_s, i_s)
        v1, v2 = v_s[:, :tm], v_s[:, tm:]
        i1, i2 = i_s[:, :tm], i_s[:, tm:]
        cands, flats, cas, cbs = [], [], [], []
        for (a, b0, b1, nvalid) in _CAND_GROUPS:
            ok = sub < nvalid
            cands.append(jnp.where(ok, v1[a:a + 1] + v2[b0:b1], -jnp.inf))
            flats.append((a * TOPK + b0 + sub).astype(F32))
            cas.append(jnp.broadcast_to(i1[a:a + 1], (8, tm)))
            cbs.append(i2[b0:b1])
        cands.append(v1[8:16] + v2[0:1])
        flats.append(((8 + sub) * TOPK).astype(F32))
        cas.append(i1[8:16])
        cbs.append(jnp.broadcast_to(i2[0:1], (8, tm)))
        cat = lambda xs: jnp.concatenate(xs, axis=0)
        _top_rows(cat(cands), cat(flats), TOPK, sc_s, payloads=((ea_s, cat(cas)), (eb_s, cat(cbs))))
        sc = sc_s[...]
        e = jnp.exp(sc - sc[0:1])
        row = pl.ds(pl.multiple_of(h * TOPK, TOPK), TOPK)
        gt_s[row, :] = e / jnp.sum(e, axis=0, keepdims=True)
        at_s[row, :] = ea_s[...]
        bt_s[row, :] = eb_s[...]
        return carry

    lax.fori_loop(0, heads, head, 0)
    g_ref[...] = gt_s[...].T
    a_ref[...] = at_s[...].T
    b_ref[...] = bt_s[...].T


def _route(xn, wq_t, keys2d, heads, n_keys):
    t, d = xn.shape
    tm = _pick_tile(t, 256, LANES)
    hk = heads * TOPK
    out = jax.ShapeDtypeStruct((t, hk), F32)
    ospec = pl.BlockSpec((tm, hk), lambda i: (i, 0))
    tmp = lambda w: pltpu.VMEM((TOPK, w), F32)
    return pl.pallas_call(
        functools.partial(_route_body, heads=heads, n_keys=n_keys),
        grid=(t // tm,),
        in_specs=[pl.BlockSpec((tm, d), lambda i: (i, 0)),
                  pl.BlockSpec(wq_t.shape, lambda i: (0, 0)),
                  pl.BlockSpec(keys2d.shape, lambda i: (0, 0))],
        out_specs=[ospec, ospec, ospec],
        out_shape=[out, out, out],
        scratch_shapes=[pltpu.VMEM((wq_t.shape[0], tm), BF16)] + [pltpu.VMEM((hk, tm), F32) for _ in range(3)]
        + [tmp(2 * tm), tmp(2 * tm), tmp(tm), tmp(tm), tmp(tm)],
        compiler_params=_cparams(("parallel",)),
        name="peer_route",
    )(xn, wq_t, keys2d)


def _gelu(x):
    return 0.5 * x * (1.0 + lax.erf(x * (2.0 ** -0.5)))


W_PITCH_PAD = 8
MXU_DIM = 256


def _unpack_bf16_row(w, odd):
    bits = (w & jnp.uint32(0xFFFF0000)) if odd else lax.shift_left(w, jnp.uint32(16))
    return lax.bitcast_convert_type(bits, F32)


def _expert_body(xn_ref, res_ref, g_ref, a_ref, b_ref, ut_ref, v_ref, o_ref, w_s, acc_s, *, n_keys):
    j = pl.program_id(1)
    tm = xn_ref.shape[0]
    te = ut_ref.shape[1]
    pitch = n_keys // 2 + W_PITCH_PAD

    @pl.when(j == 0)
    def _():
        acc_s[...] = jnp.zeros_like(acc_s)
        key = lax.broadcasted_iota(I32, (n_keys, g_ref.shape[1]), 0).astype(BF16)

        def token(t, carry):
            a_row = a_ref[pl.ds(t, 1), :].astype(BF16)
            b_row = b_ref[pl.ds(t, 1), :].astype(BF16)
            g_row = g_ref[pl.ds(t, 1), :].astype(BF16)
            pt = jnp.where(key == a_row, g_row, jnp.zeros_like(g_row))
            qt = jnp.where(key == b_row, jnp.ones_like(g_row), jnp.zeros_like(g_row))
            w_t = _nt_dot(pt, qt).astype(BF16)
            w_s[pl.ds(pl.multiple_of(t * pitch, 8), n_keys // 2), :] = pltpu.bitcast(w_t, jnp.uint32)
            return carry

        lax.fori_loop(0, tm, token, 0, unroll=64)

    n_slab = te // n_keys
    act = jnp.dot(xn_ref[...], ut_ref[...], preferred_element_type=F32)
    coef = []
    for s in range(0, n_slab, 2):
        packed = w_s[pl.ds(j * (n_slab // 2) + s // 2, tm, stride=pitch), :]
        for odd in range(2):
            sl = slice((s + odd) * n_keys, (s + odd + 1) * n_keys)
            coef.append((_gelu(act[:, sl]) * _unpack_bf16_row(packed, odd)).astype(BF16))
    acc_s[...] += jnp.dot(jnp.concatenate(coef, axis=1), v_ref[...], preferred_element_type=F32)

    @pl.when(j == pl.num_programs(1) - 1)
    def _():
        o_ref[...] = res_ref[...] + acc_s[...]


def _experts(xn, res, g, a, b, u_t, v, n_keys):
    t, d = xn.shape
    ne = v.shape[0]
    hk = g.shape[1]
    tm = _pick_tile(t, 512, 2 * LANES)
    te = _pick_tile(ne, 2048, MXU_DIM)
    tok = lambda c: pl.BlockSpec((tm, c), lambda i, j: (i, 0))
    return pl.pallas_call(
        functools.partial(_expert_body, n_keys=n_keys),
        grid=(t // tm, ne // te),
        in_specs=[tok(d), tok(d), tok(hk), tok(hk), tok(hk),
                  pl.BlockSpec((d, te), lambda i, j: (0, j)), pl.BlockSpec((te, d), lambda i, j: (j, 0))],
        out_specs=tok(d),
        out_shape=jax.ShapeDtypeStruct((t, d), F32),
        scratch_shapes=[pltpu.VMEM(((n_keys // 2 + W_PITCH_PAD) * tm, n_keys), jnp.uint32),
                        pltpu.VMEM((tm, d), F32)],
        compiler_params=_cparams(("parallel", "arbitrary")),
        name="peer_experts",
    )(xn, res, g, a, b, u_t, v)


def _layer(x, meta_tokens, norm_mix, w_in, conv_w, a_log, dt_bias, o_norm_a, q_norm_b, k_norm_b, f_bias,
           w_branch, w_out, norm_ffn, peer_wq, peer_sub_keys, expert_u, expert_v):
    batch, seq, d = x.shape
    n_meta = meta_tokens.shape[0]
    pad = BLOCK - n_meta
    lp = BLOCK + seq
    ha, hb = a_log.shape[0], f_bias.shape[0]
    aqk = ha * LANES
    bw = hb * LANES
    assert ha <= 8 and hb <= 8 and seq % BLOCK == 0 and conv_w.shape == (CONV_K, 3 * aqk)
    assert w_in.shape[1] == 4 * aqk + 2 * ha + 3 * bw + hb + 2 * d
    c_z, c_b, c_a = 3 * aqk, 4 * aqk, 4 * aqk + ha
    c_qkvb = 4 * aqk + 2 * ha
    c_f = c_qkvb + 3 * bw
    c_g = c_f + hb

    x2d = x.reshape(batch * seq, d)
    hn_x = _rmsnorm(x2d, norm_mix, BF16)
    hn_m = _rmsnorm(meta_tokens, norm_mix, BF16)
    hn_p = jnp.concatenate([jnp.zeros((batch, pad, d), BF16), jnp.broadcast_to(hn_m[None], (batch, n_meta, d)),
                            hn_x.reshape(batch, seq, d)], axis=1).reshape(batch * lp, d)

    wb16 = w_in.astype(BF16)
    qkv_a = _proj(_proj_plain_body, hn_p, wb16[:, :c_z], F32, _pick_tile(c_z, 1024, LANES), name="proj_qkv_a")
    zs = _proj(_proj_silu_body, hn_x, wb16[:, c_z:c_b], BF16, _pick_tile(aqk, 1024, LANES), name="proj_z")
    scale = LANES ** -0.5
    nw = jnp.concatenate([jnp.tile(q_norm_b * scale, hb), jnp.tile(k_norm_b, hb), jnp.ones((bw,), F32)]).reshape(1, 3 * bw)
    qkv_b = _proj(_proj_qkvb_body, hn_p, wb16[:, c_qkvb:c_f], BF16, bw, extra=(nw,), name="proj_qkv_b")
    gates = _proj(_proj_sigmoid_body, hn_x, wb16[:, c_g:], BF16, _pick_tile(2 * d, 1024, LANES), name="proj_gates")

    def rows8(w):
        return jnp.pad(w.T, ((0, 8 - w.shape[1]), (0, 0)))

    w_small = jnp.concatenate([rows8(wb16[:, c_b:c_a]), rows8(wb16[:, c_a:c_qkvb]), rows8(wb16[:, c_f:c_g]),
                               jnp.zeros((SMALL_ROWS - 24, d), BF16)], axis=0)
    small_t = _proj_small_t(hn_p, w_small)

    def lanes8(p):
        return jnp.broadcast_to(jnp.pad(p.astype(F32), (0, 8 - p.shape[0]))[:, None], (8, LANES))

    beta, gc, cumf = _prep(small_t, lanes8(a_log), lanes8(dt_bias), lanes8(f_bias), batch, lp, pad)
    n_chunks = lp // CHUNK
    beta5 = beta.reshape(batch, 8, n_chunks, 1, CHUNK)
    gc5 = gc.reshape(batch, 8, n_chunks, 1, CHUNK)
    u, w, qd, kt, at = _wy(qkv_a.reshape(batch, lp, 3 * aqk), conv_w, beta5, gc5, batch, lp, ha)
    o_a = _scan(u, w, qd, kt, at, gc5, zs.reshape(batch, seq, aqk), o_norm_a.reshape(1, LANES).astype(F32),
                batch, lp, ha)

    o_b = _fox(qkv_b.reshape(batch, lp, 3 * bw), cumf.reshape(batch * 8, 1, lp), batch, lp, hb, pad)

    res, xn = _merge(o_a.reshape(batch * seq, aqk), o_b.reshape(batch * seq, bw), gates, x2d,
                     w_branch[0].astype(BF16), w_branch[1].astype(BF16), w_out.astype(BF16),
                     norm_ffn.reshape(1, d).astype(F32))

    hp, n_keys = peer_sub_keys.shape[1], peer_sub_keys.shape[2]
    assert peer_sub_keys.shape[3] == LANES and n_keys == LANES and peer_wq.shape[1] == hp * 2 * LANES
    g_w, a_k, b_k = _route(xn, peer_wq.T.astype(BF16), peer_sub_keys.reshape(2 * hp * n_keys, LANES).astype(BF16),
                           hp, n_keys)
    out = _experts(xn, res, g_w, a_k, b_k, expert_u.T.astype(BF16), expert_v.astype(BF16), n_keys)
    return out.reshape(batch, seq, d)


def kernel(x, meta_tokens, norm_mix, w_in, conv_w, a_log, dt_bias, o_norm_a, q_norm_b, k_norm_b, f_bias,
           w_branch, w_out, norm_ffn, peer_wq, peer_sub_keys, expert_u, expert_v):
    assert norm_mix.shape[0] == 1, "single-layer block"
    return _layer(x, meta_tokens, norm_mix[0], w_in[0], conv_w[0], a_log[0], dt_bias[0], o_norm_a[0],
                  q_norm_b[0], k_norm_b[0], f_bias[0], w_branch[0], w_out[0], norm_ffn[0], peer_wq[0],
                  peer_sub_keys[0], expert_u[0], expert_v[0])
```

```python
import functools
import math

import jax
import jax.numpy as jnp
from jax import lax
from jax.experimental import pallas as pl
from jax.experimental.pallas import tpu as pltpu

F32, BF16, I32 = jnp.float32, jnp.bfloat16, jnp.int32
EPS = 1e-6
NEG_INF = -1e30
LANES = 128
BLOCK = 128
CHUNK = 64
CONV_K = 4
TOPK = 16
SMALL_ROWS = 32
VMEM_LIMIT = 56 * 1024 * 1024


def _cparams(sem):
    return pltpu.CompilerParams(dimension_semantics=sem, vmem_limit_bytes=VMEM_LIMIT)


def _pick_tile(n, target, mult):
    best = None
    for t in range(mult, min(n, target) + 1, mult):
        if n % t == 0:
            best = t
    assert best is not None, (n, target, mult)
    return best


def _nt_dot(a, b):
    return lax.dot_general(a, b, (((1,), (1,)), ((), ())), preferred_element_type=F32)


def _tn_dot(a, b):
    return lax.dot_general(a, b, (((0,), (0,)), ((), ())), preferred_element_type=F32)


def _split3(x):
    hi = x.astype(BF16)
    r = x - hi.astype(F32)
    mid = r.astype(BF16)
    lo = (r - mid.astype(F32)).astype(BF16)
    return hi, mid, lo


def _dot_01(x, m01):
    hi, mid, lo = _split3(x)
    d = functools.partial(jnp.dot, preferred_element_type=F32)
    return d(hi, m01) + (d(mid, m01) + d(lo, m01))


def _sigmoid(x):
    return 1.0 / (1.0 + jnp.exp(-x))


def _softplus(x):
    return jnp.maximum(x, 0.0) + jnp.log1p(jnp.exp(-jnp.abs(x)))


def _rmsnorm_body(x_ref, w_ref, o_ref):
    x = x_ref[...].astype(F32)
    y = x * lax.rsqrt(jnp.mean(x * x, axis=-1, keepdims=True) + EPS)
    o_ref[...] = (y * w_ref[...]).astype(o_ref.dtype)


def _rmsnorm(x2d, w, out_dtype):
    m, d = x2d.shape
    tm = _pick_tile(m, 1024, 8)
    return pl.pallas_call(
        _rmsnorm_body,
        grid=(m // tm,),
        in_specs=[pl.BlockSpec((tm, d), lambda i: (i, 0)), pl.BlockSpec((1, d), lambda i: (0, 0))],
        out_specs=pl.BlockSpec((tm, d), lambda i: (i, 0)),
        out_shape=jax.ShapeDtypeStruct((m, d), out_dtype),
        compiler_params=_cparams(("parallel",)),
        name="rmsnorm",
    )(x2d, w.reshape(1, d).astype(F32))


def _proj_plain_body(x_ref, w_ref, o_ref):
    o_ref[...] = jnp.dot(x_ref[...], w_ref[...], preferred_element_type=F32).astype(o_ref.dtype)


def _row_halves(x_ref, w_ref):
    th = x_ref.shape[0] // 2
    rows = [slice(0, th), slice(th, 2 * th)]
    return [(r, jnp.dot(x_ref[r, :], w_ref[...], preferred_element_type=F32)) for r in rows]


def _proj_silu_body(x_ref, w_ref, o_ref):
    for r, y in _row_halves(x_ref, w_ref):
        o_ref[r, :] = (y * _sigmoid(y)).astype(o_ref.dtype)


def _proj_sigmoid_body(x_ref, w_ref, o_ref):
    for r, y in _row_halves(x_ref, w_ref):
        o_ref[r, :] = _sigmoid(y).astype(o_ref.dtype)


def _proj_qkvb_body(x_ref, w_ref, nw_ref, o_ref):
    do_norm = pl.program_id(1) < 2
    for r, y in _row_halves(x_ref, w_ref):
        for h in range(y.shape[1] // LANES):
            sl = slice(h * LANES, (h + 1) * LANES)
            seg = y[:, sl]
            rs = lax.rsqrt(jnp.mean(seg * seg, axis=-1, keepdims=True) + EPS)
            rs = jnp.where(do_norm, rs, 1.0)
            o_ref[r, sl] = (seg * rs * nw_ref[:, sl]).astype(o_ref.dtype)


def _proj(body, x2d, w, out_dtype, tn, extra=(), name="proj"):
    m, k = x2d.shape
    n = w.shape[1]
    tm = _pick_tile(m, 1536, 128)
    assert n % tn == 0
    in_specs = [pl.BlockSpec((tm, k), lambda i, j: (i, 0)), pl.BlockSpec((k, tn), lambda i, j: (0, j))]
    in_specs += [pl.BlockSpec((1, tn), lambda i, j: (0, j)) for _ in extra]
    return pl.pallas_call(
        body,
        grid=(m // tm, n // tn),
        in_specs=in_specs,
        out_specs=pl.BlockSpec((tm, tn), lambda i, j: (i, j)),
        out_shape=jax.ShapeDtypeStruct((m, n), out_dtype),
        compiler_params=_cparams(("parallel", "arbitrary")),
        name=name,
    )(x2d, w, *extra)


def _small_body(w_ref, x_ref, o_ref):
    o_ref[...] = _nt_dot(w_ref[...], x_ref[...])


def _proj_small_t(x2d, w_t):
    m, k = x2d.shape
    tm = _pick_tile(m, 2048, 128)
    return pl.pallas_call(
        _small_body,
        grid=(m // tm,),
        in_specs=[pl.BlockSpec((SMALL_ROWS, k), lambda i: (0, 0)), pl.BlockSpec((tm, k), lambda i: (i, 0))],
        out_specs=pl.BlockSpec((SMALL_ROWS, tm), lambda i: (0, i)),
        out_shape=jax.ShapeDtypeStruct((SMALL_ROWS, m), F32),
        compiler_params=_cparams(("parallel",)),
        name="proj_small",
    )(w_t, x2d)


def _prep_body(s_ref, alog_ref, dtb_ref, fb_ref, beta_ref, gc_ref, cumf_ref, carry_ref, *, pad):
    j = pl.program_id(1)

    @pl.when(j == 0)
    def _():
        carry_ref[...] = jnp.zeros_like(carry_ref)

    tl = s_ref.shape[1]
    sm = s_ref[...]
    b, a, f = sm[0:8], sm[8:16], sm[16:24]
    pos = j * tl + lax.broadcasted_iota(I32, (8, tl), 1)
    valid = pos >= pad
    bc = lambda p_ref: p_ref[:, 0:1]
    beta = jnp.where(valid, _sigmoid(b), 0.0)
    g = jnp.where(valid, -jnp.exp(bc(alog_ref)) * _softplus(a + bc(dtb_ref)), 0.0)
    logf = jnp.where(valid, -_softplus(-(f + bc(fb_ref))), 0.0)
    r = lax.broadcasted_iota(I32, (tl, tl), 0)
    c = lax.broadcasted_iota(I32, (tl, tl), 1)
    upper = r <= c
    shift = CHUNK.bit_length() - 1
    same_chunk = lax.shift_right_logical(r, shift) == lax.shift_right_logical(c, shift)
    m_seq = jnp.where(upper, 1.0, 0.0).astype(BF16)
    m_chunk = jnp.where(upper & same_chunk, 1.0, 0.0).astype(BF16)
    beta_ref[0] = beta
    gc_ref[0] = _dot_01(g, m_chunk)
    cum = _dot_01(logf, m_seq) + carry_ref[:, 0:1]
    cumf_ref[0] = cum
    carry_ref[...] = jnp.broadcast_to(cum[:, tl - 1:tl], (8, LANES))


def _prep(small_t, alog, dtb, fb, batch, lp, pad):
    tl = _pick_tile(lp, 512, LANES)
    nb = lp // tl
    out = jax.ShapeDtypeStruct((batch, 8, lp), F32)
    pspec = pl.BlockSpec((8, LANES), lambda b, j: (0, 0))
    ospec = pl.BlockSpec((1, 8, tl), lambda b, j: (b, 0, j))
    return pl.pallas_call(
        functools.partial(_prep_body, pad=pad),
        grid=(batch, nb),
        in_specs=[pl.BlockSpec((SMALL_ROWS, tl), lambda b, j: (0, b * nb + j)), pspec, pspec, pspec],
        out_specs=[ospec, ospec, ospec],
        out_shape=[out, out, out],
        scratch_shapes=[pltpu.VMEM((8, LANES), F32)],
        compiler_params=_cparams(("parallel", "arbitrary")),
        name="gate_prep",
    )(small_t, alog, dtb, fb)


SUB = 16
HALO_ROWS = 16


def _bmm(a, b):
    return jnp.einsum('cij,cjk->cik', a.astype(BF16), b.astype(BF16), preferred_element_type=F32)


def _bmm_nt(a, b):
    return jnp.einsum('cik,cjk->cij', a.astype(BF16), b.astype(BF16), preferred_element_type=F32)


def _wy_body(x_ref, halo_ref, cw_ref, beta_ref, gc_ref, u_ref, w_ref, qd_ref, kt_ref, at_ref, *, heads, cb):
    rows = cb * CHUNK
    aqk = heads * LANES
    halo = halo_ref[0].astype(F32)[HALO_ROWS - 8:]
    xw = jnp.concatenate([halo, x_ref[0].astype(F32)], axis=0)
    cw = cw_ref[...]
    y = cw[0:1] * xw[5:5 + rows]
    for i in range(1, CONV_K):
        y = y + cw[i:i + 1] * xw[5 + i:5 + i + rows]
    y = y * _sigmoid(y)
    ii = lax.broadcasted_iota(I32, (CHUNK, CHUNK), 0)
    jj = lax.broadcasted_iota(I32, (CHUNK, CHUNK), 1)
    eye = ii == jj
    incl = ii >= jj
    sub_shift = SUB.bit_length() - 1
    same_sub = lax.shift_right_logical(ii, sub_shift) == lax.shift_right_logical(jj, sub_shift)
    eye_f = jnp.where(eye, 1.0, 0.0)

    def heads_batched(off):
        return jnp.concatenate([y[:, off + h * LANES:off + (h + 1) * LANES].reshape(cb, CHUNK, LANES)
                                for h in range(heads)], axis=0)

    q, k, v = heads_batched(0), heads_batched(aqk), heads_batched(2 * aqk)
    q = q * lax.rsqrt(jnp.sum(q * q, axis=-1, keepdims=True) + EPS) * (LANES ** -0.5)
    k = k * lax.rsqrt(jnp.sum(k * k, axis=-1, keepdims=True) + EPS)
    b_row = jnp.concatenate([beta_ref[0, h] for h in range(heads)], axis=0)
    g_row = jnp.concatenate([gc_ref[0, h] for h in range(heads)], axis=0)
    b_col = jnp.sum(jnp.where(eye, b_row, 0.0), axis=2, keepdims=True)
    g_col = jnp.sum(jnp.where(eye, g_row, 0.0), axis=2, keepdims=True)
    gamma = jnp.where(incl, jnp.exp(jnp.where(incl, g_col - g_row, 0.0)), 0.0)
    kb = k * b_col
    a_mat = jnp.where(ii > jj, _bmm_nt(kb, k) * gamma, 0.0)
    attn = _bmm_nt(q, k) * gamma
    x = jnp.where(same_sub, -a_mat, 0.0)
    t_d = eye_f + x
    for _ in range(3):
        x = _bmm(x, x)
        t_d = t_d + _bmm(t_d, x)
    m1 = _bmm(t_d, jnp.where(same_sub, 0.0, a_mat))
    m2 = _bmm(m1, m1)
    t = _bmm(eye_f - m1 + m2 - _bmm(m1, m2), t_d)
    e_g = jnp.exp(g_col)
    sol = _bmm(t, jnp.concatenate([v * b_col, kb * e_g], axis=2))
    g_last = g_row[:, :, CHUNK - 1:CHUNK]
    qd = q * e_g
    kt = k * jnp.exp(g_last - g_col)
    at = jnp.concatenate([attn, jnp.zeros_like(attn)], axis=2)
    for h in range(heads):
        hs = slice(h * LANES, (h + 1) * LANES)
        bs = slice(h * cb, (h + 1) * cb)
        u_ref[0, :, hs] = sol[bs, :, :LANES].reshape(rows, LANES)
        w_ref[0, :, hs] = sol[bs, :, LANES:].reshape(rows, LANES).astype(BF16)
        qd_ref[0, :, hs] = qd[bs].reshape(rows, LANES).astype(BF16)
        kt_ref[0, :, hs] = kt[bs].reshape(rows, LANES).astype(BF16)
        at_ref[0, :, hs] = at[bs].reshape(rows, LANES).astype(BF16)


def _wy(qkv_a, conv_w, beta5, gc5, batch, lp, heads):
    n_chunks = lp // CHUNK
    cb = max(c for c in range(1, 7) if n_chunks % c == 0)
    rows = cb * CHUNK
    width = heads * LANES
    ospec = pl.BlockSpec((1, rows, width), lambda b, i: (b, i, 0))
    rspec = pl.BlockSpec((1, 8, cb, 1, CHUNK), lambda b, i: (b, 0, i, 0, 0))
    f32o = jax.ShapeDtypeStruct((batch, lp, width), F32)
    b16o = jax.ShapeDtypeStruct((batch, lp, width), BF16)
    return pl.pallas_call(
        functools.partial(_wy_body, heads=heads, cb=cb),
        grid=(batch, n_chunks // cb),
        in_specs=[pl.BlockSpec((1, rows, 3 * width), lambda b, i: (b, i, 0)),
                  pl.BlockSpec((1, HALO_ROWS, 3 * width),
                               lambda b, i: (b, jnp.maximum(i * (rows // HALO_ROWS) - 1, 0), 0)),
                  pl.BlockSpec((CONV_K, 3 * width), lambda b, i: (0, 0)), rspec, rspec],
        out_specs=[ospec] * 5,
        out_shape=[f32o, b16o, b16o, b16o, b16o],
        compiler_params=_cparams(("parallel", "parallel")),
        name="delta_rule_wy",
    )(qkv_a, qkv_a, conv_w, beta5, gc5)


def _scan_body(u_ref, w_ref, qd_ref, kt_ref, at_ref, gc_ref, zs_ref, onw_ref, o_ref, s_s, *, heads):
    @pl.when(pl.program_id(1) == 0)
    def _():
        s_s[...] = jnp.zeros_like(s_s)

    hsl = [slice(h * LANES, (h + 1) * LANES) for h in range(heads)]
    for c in range(BLOCK // CHUNK):
        rs = slice(c * CHUNK, (c + 1) * CHUNK)
        s = [s_s[h] for h in range(heads)]
        ws = [jnp.dot(jnp.concatenate([w_ref[0, rs, hs], qd_ref[0, rs, hs]], axis=0), s[h].astype(BF16),
                      preferred_element_type=F32) for h, hs in enumerate(hsl)]
        vb = [(u_ref[0, rs, hs] - ws[h][:CHUNK]).astype(BF16) for h, hs in enumerate(hsl)]
        upd = [_tn_dot(kt_ref[0, rs, hs], vb[h]) for h, hs in enumerate(hsl)]
        o = [ws[h][CHUNK:] + jnp.dot(at_ref[0, rs, h * LANES:h * LANES + CHUNK], vb[h], preferred_element_type=F32)
             for h in range(heads)]
        for h, hs in enumerate(hsl):
            dec = jnp.exp(gc_ref[0, h, c][:, CHUNK - 1:CHUNK])
            s_s[h] = s[h] * dec + upd[h]
            on = o[h] * lax.rsqrt(jnp.mean(o[h] * o[h], axis=-1, keepdims=True) + EPS) * onw_ref[...]
            o_ref[0, rs, hs] = (on * zs_ref[0, rs, hs].astype(F32)).astype(o_ref.dtype)


def _scan(u, w, qd, kt, at, gc5, zs, onw, batch, lp, heads):
    seq = lp - BLOCK
    width = heads * LANES
    ispec = pl.BlockSpec((1, BLOCK, width), lambda b, i: (b, i, 0))
    xspec = pl.BlockSpec((1, BLOCK, width), lambda b, i: (b, jnp.maximum(i - 1, 0), 0))
    return pl.pallas_call(
        functools.partial(_scan_body, heads=heads),
        grid=(batch, lp // BLOCK),
        in_specs=[ispec] * 5 + [pl.BlockSpec((1, 8, BLOCK // CHUNK, 1, CHUNK), lambda b, i: (b, 0, i, 0, 0)),
                                xspec, pl.BlockSpec((1, LANES), lambda b, i: (0, 0))],
        out_specs=xspec,
        out_shape=jax.ShapeDtypeStruct((batch, seq, width), BF16),
        scratch_shapes=[pltpu.VMEM((heads, LANES, LANES), F32)],
        compiler_params=_cparams(("parallel", "arbitrary")),
        name="delta_rule_scan",
    )(u, w, qd, kt, at, gc5, zs, onw)


def _fox_body(q_ref, k_ref, v_ref, cum_ref, o_ref, *, tq, pad):
    i = pl.program_id(2)
    q0 = pl.multiple_of(BLOCK + i * tq, BLOCK)
    th = tq // 2
    q = q_ref[0, pl.ds(q0, tq), :]

    def tile(qx, k0, size, mask, carry):
        m, l, acc = carry
        s = _nt_dot(qx, k_ref[0, pl.ds(k0, size), :]) - cum_ref[0, :, pl.ds(k0, size)]
        if mask is not None:
            s = jnp.where(mask, s, NEG_INF)
        m_new = jnp.maximum(m, jnp.max(s, axis=1, keepdims=True))
        alpha = jnp.exp(m - m_new)
        p = jnp.exp(s - m_new)
        l = alpha * l + jnp.sum(p, axis=1, keepdims=True)
        acc = alpha * acc + jnp.dot(p.astype(BF16), v_ref[0, pl.ds(k0, size), :], preferred_element_type=F32)
        return m_new, l, acc

    carry = (jnp.full((tq, 1), NEG_INF, F32), jnp.zeros((tq, 1), F32), jnp.zeros((tq, LANES), F32))
    carry = tile(q, 0, BLOCK, lax.broadcasted_iota(I32, (tq, BLOCK), 1) >= pad, carry)
    carry = lax.fori_loop(0, i, lambda j, c: tile(q, pl.multiple_of(BLOCK + j * tq, BLOCK), tq, None, c), carry)
    m, l, acc = tile(q, q0, th, lax.broadcasted_iota(I32, (tq, th), 1) <= lax.broadcasted_iota(I32, (tq, th), 0), carry)
    o_ref[0, :th, :] = (acc[:th] / l[:th]).astype(o_ref.dtype)
    causal = lax.broadcasted_iota(I32, (th, th), 1) <= lax.broadcasted_iota(I32, (th, th), 0)
    m, l, acc = tile(q[th:], q0 + th, th, causal, (m[th:], l[th:], acc[th:]))
    o_ref[0, th:, :] = (acc / l).astype(o_ref.dtype)


def _fox(qkv, cumf3, batch, lp, heads, pad):
    seq = lp - BLOCK
    tq = _pick_tile(seq, 1024, 2 * BLOCK)
    kvspec = lambda off: pl.BlockSpec((1, lp, LANES), lambda b, h, i: (b, 0, off + h))
    return pl.pallas_call(
        functools.partial(_fox_body, tq=tq, pad=pad),
        grid=(batch, heads, seq // tq),
        in_specs=[kvspec(0), kvspec(heads), kvspec(2 * heads),
                  pl.BlockSpec((1, 1, lp), lambda b, h, i: (b * 8 + h, 0, 0))],
        out_specs=pl.BlockSpec((1, tq, LANES), lambda b, h, i: (b, i, h)),
        out_shape=jax.ShapeDtypeStruct((batch, seq, heads * LANES), BF16),
        compiler_params=_cparams(("parallel", "parallel", "arbitrary")),
        name="forgetting_attention",
    )(qkv, qkv, qkv, cumf3)


def _merge_body(oa_ref, ob_ref, g_ref, x_ref, wa_ref, wb_ref, wo_ref, nw_ref, res_ref, xn_ref):
    d = x_ref.shape[1]
    ya = jnp.dot(oa_ref[...], wa_ref[...], preferred_element_type=F32)
    yb = jnp.dot(ob_ref[...], wb_ref[...], preferred_element_type=F32)
    g = g_ref[...].astype(F32)
    mix = g[:, :d] * ya + g[:, d:] * yb
    res = x_ref[...] + jnp.dot(mix.astype(BF16), wo_ref[...], preferred_element_type=F32)
    res_ref[...] = res
    xn = res * lax.rsqrt(jnp.mean(res * res, axis=-1, keepdims=True) + EPS) * nw_ref[...]
    xn_ref[...] = xn.astype(xn_ref.dtype)


def _merge(oa, ob, gates, x2d, wa, wb, wo, nw):
    m, d = x2d.shape
    bw = oa.shape[1]
    tm = _pick_tile(m, 512, 128)
    row = lambda c: pl.BlockSpec((tm, c), lambda i: (i, 0))
    full = lambda r, c: pl.BlockSpec((r, c), lambda i: (0, 0))
    return pl.pallas_call(
        _merge_body,
        grid=(m // tm,),
        in_specs=[row(bw), row(bw), row(2 * d), row(d), full(bw, d), full(bw, d), full(d, d), full(1, d)],
        out_specs=[row(d), row(d)],
        out_shape=[jax.ShapeDtypeStruct((m, d), F32), jax.ShapeDtypeStruct((m, d), BF16)],
        compiler_params=_cparams(("parallel",)),
        name="merge_out_proj",
    )(oa, ob, gates, x2d, wa, wb, wo, nw)


_CAND_GROUPS = ((0, 0, 8, 8), (0, 8, 16, 8), (1, 0, 8, 8), (2, 0, 8, 5), (3, 0, 8, 4), (4, 0, 8, 3),
                (5, 0, 8, 2), (6, 0, 8, 2), (7, 0, 8, 2))


def _top_rows(x, key, n, val_ref, key_ref=None, payloads=()):
    big = jnp.float32(1e9)
    for kk in range(n):
        m = jnp.max(x, axis=0, keepdims=True)
        pos = jnp.min(jnp.where(x == m, key, big), axis=0, keepdims=True)
        sel = key == pos
        val_ref[kk:kk + 1, :] = m
        if key_ref is not None:
            key_ref[kk:kk + 1, :] = pos
        for ref, pay in payloads:
            ref[kk:kk + 1, :] = jnp.sum(jnp.where(sel, pay, 0.0), axis=0, keepdims=True)
        x = jnp.where(sel, -jnp.inf, x)


def _route_body(xn_ref, wqt_ref, keys_ref, g_ref, a_ref, b_ref, qt_s, gt_s, at_s, bt_s, v_s, i_s, sc_s, ea_s, eb_s,
                *, heads, n_keys):
    tm = xn_ref.shape[0]
    qt_s[...] = _nt_dot(wqt_ref[...], xn_ref[...]).astype(BF16)
    key_iota = lax.broadcasted_iota(I32, (n_keys, 2 * tm), 0).astype(F32)
    sub = lax.broadcasted_iota(I32, (8, tm), 0)

    def head(h, carry):
        st = []
        for p in range(2):
            qhp = qt_s[pl.ds(pl.multiple_of(h * 2 * LANES + p * LANES, LANES), LANES), :]
            kk = keys_ref[pl.ds(pl.multiple_of((p * heads + h) * n_keys, n_keys), n_keys), :]
            st.append(jnp.dot(kk, qhp, preferred_element_type=F32))
        _top_rows(jnp.concatenate(st, axis=1), key_iota, TOPK, v_s, i_s)
        v1, v2 = v_s[:, :tm], v_s[:, tm:]
        i1, i2 = i_s[:, :tm], i_s[:, tm:]
        cands, flats, cas, cbs = [], [], [], []
        for (a, b0, b1, nvalid) in _CAND_GROUPS:
            ok = sub < nvalid
            cands.append(jnp.where(ok, v1[a:a + 1] + v2[b0:b1], -jnp.inf))
            flats.append((a * TOPK + b0 + sub).astype(F32))
            cas.append(jnp.broadcast_to(i1[a:a + 1], (8, tm)))
            cbs.append(i2[b0:b1])
        cands.append(v1[8:16] + v2[0:1])
        flats.append(((8 + sub) * TOPK).astype(F32))
        cas.append(i1[8:16])
        cbs.append(jnp.broadcast_to(i2[0:1], (8, tm)))
        cat = lambda xs: jnp.concatenate(xs, axis=0)
        _top_rows(cat(cands), cat(flats), TOPK, sc_s, payloads=((ea_s, cat(cas)), (eb_s, cat(cbs))))
        sc = sc_s[...]
        e = jnp.exp(sc - sc[0:1])
        row = pl.ds(pl.multiple_of(h * TOPK, TOPK), TOPK)
        gt_s[row, :] = e / jnp.sum(e, axis=0, keepdims=True)
        at_s[row, :] = ea_s[...]
        bt_s[row, :] = eb_s[...]
        return carry

    lax.fori_loop(0, heads, head, 0)
    g_ref[...] = gt_s[...].T
    a_ref[...] = at_s[...].T
    b_ref[...] = bt_s[...].T


def _route(xn, wq_t, keys2d, heads, n_keys):
    t, d = xn.shape
    tm = _pick_tile(t, 256, LANES)
    hk = heads * TOPK
    out = jax.ShapeDtypeStruct((t, hk), F32)
    ospec = pl.BlockSpec((tm, hk), lambda i: (i, 0))
    tmp = lambda w: pltpu.VMEM((TOPK, w), F32)
    return pl.pallas_call(
        functools.partial(_route_body, heads=heads, n_keys=n_keys),
        grid=(t // tm,),
        in_specs=[pl.BlockSpec((tm, d), lambda i: (i, 0)),
                  pl.BlockSpec(wq_t.shape, lambda i: (0, 0)),
                  pl.BlockSpec(keys2d.shape, lambda i: (0, 0))],
        out_specs=[ospec, ospec, ospec],
        out_shape=[out, out, out],
        scratch_shapes=[pltpu.VMEM((wq_t.shape[0], tm), BF16)] + [pltpu.VMEM((hk, tm), F32) for _ in range(3)]
        + [tmp(2 * tm), tmp(2 * tm), tmp(tm), tmp(tm), tmp(tm)],
        compiler_params=_cparams(("parallel",)),
        name="peer_route",
    )(xn, wq_t, keys2d)


def _gelu(x):
    return 0.5 * x * (1.0 + lax.erf(x * (2.0 ** -0.5)))


W_PITCH_PAD = 8
MXU_DIM = 256


def _unpack_bf16_row(w, odd):
    bits = (w & jnp.uint32(0xFFFF0000)) if odd else lax.shift_left(w, jnp.uint32(16))
    return lax.bitcast_convert_type(bits, F32)


def _expert_body(xn_ref, res_ref, g_ref, a_ref, b_ref, ut_ref, v_ref, o_ref, w_s, acc_s, *, n_keys):
    j = pl.program_id(1)
    tm = xn_ref.shape[0]
    te = ut_ref.shape[1]
    pitch = n_keys // 2 + W_PITCH_PAD

    @pl.when(j == 0)
    def _():
        acc_s[...] = jnp.zeros_like(acc_s)
        key = lax.broadcasted_iota(I32, (n_keys, g_ref.shape[1]), 0).astype(BF16)

        def token(t, carry):
            a_row = a_ref[pl.ds(t, 1), :].astype(BF16)
            b_row = b_ref[pl.ds(t, 1), :].astype(BF16)
            g_row = g_ref[pl.ds(t, 1), :].astype(BF16)
            pt = jnp.where(key == a_row, g_row, jnp.zeros_like(g_row))
            qt = jnp.where(key == b_row, jnp.ones_like(g_row), jnp.zeros_like(g_row))
            w_t = _nt_dot(pt, qt).astype(BF16)
            w_s[pl.ds(pl.multiple_of(t * pitch, 8), n_keys // 2), :] = pltpu.bitcast(w_t, jnp.uint32)
            return carry

        lax.fori_loop(0, tm, token, 0, unroll=64)

    n_slab = te // n_keys
    act = jnp.dot(xn_ref[...], ut_ref[...], preferred_element_type=F32)
    coef = []
    for s in range(0, n_slab, 2):
        packed = w_s[pl.ds(j * (n_slab // 2) + s // 2, tm, stride=pitch), :]
        for odd in range(2):
            sl = slice((s + odd) * n_keys, (s + odd + 1) * n_keys)
            coef.append((_gelu(act[:, sl]) * _unpack_bf16_row(packed, odd)).astype(BF16))
    acc_s[...] += jnp.dot(jnp.concatenate(coef, axis=1), v_ref[...], preferred_element_type=F32)

    @pl.when(j == pl.num_programs(1) - 1)
    def _():
        o_ref[...] = res_ref[...] + acc_s[...]


def _experts(xn, res, g, a, b, u_t, v, n_keys):
    t, d = xn.shape
    ne = v.shape[0]
    hk = g.shape[1]
    tm = _pick_tile(t, 512, 2 * LANES)
    te = _pick_tile(ne, 2048, MXU_DIM)
    tok = lambda c: pl.BlockSpec((tm, c), lambda i, j: (i, 0))
    return pl.pallas_call(
        functools.partial(_expert_body, n_keys=n_keys),
        grid=(t // tm, ne // te),
        in_specs=[tok(d), tok(d), tok(hk), tok(hk), tok(hk),
                  pl.BlockSpec((d, te), lambda i, j: (0, j)), pl.BlockSpec((te, d), lambda i, j: (j, 0))],
        out_specs=tok(d),
        out_shape=jax.ShapeDtypeStruct((t, d), F32),
        scratch_shapes=[pltpu.VMEM(((n_keys // 2 + W_PITCH_PAD) * tm, n_keys), jnp.uint32),
                        pltpu.VMEM((tm, d), F32)],
        compiler_params=_cparams(("parallel", "arbitrary")),
        name="peer_experts",
    )(xn, res, g, a, b, u_t, v)


def _layer(x, meta_tokens, norm_mix, w_in, conv_w, a_log, dt_bias, o_norm_a, q_norm_b, k_norm_b, f_bias,
           w_branch, w_out, norm_ffn, peer_wq, peer_sub_keys, expert_u, expert_v):
    batch, seq, d = x.shape
    n_meta = meta_tokens.shape[0]
    pad = BLOCK - n_meta
    lp = BLOCK + seq
    ha, hb = a_log.shape[0], f_bias.shape[0]
    aqk = ha * LANES
    bw = hb * LANES
    assert ha <= 8 and hb <= 8 and seq % BLOCK == 0 and conv_w.shape == (CONV_K, 3 * aqk)
    assert w_in.shape[1] == 4 * aqk + 2 * ha + 3 * bw + hb + 2 * d
    c_z, c_b, c_a = 3 * aqk, 4 * aqk, 4 * aqk + ha
    c_qkvb = 4 * aqk + 2 * ha
    c_f = c_qkvb + 3 * bw
    c_g = c_f + hb

    x2d = x.reshape(batch * seq, d)
    hn_x = _rmsnorm(x2d, norm_mix, BF16)
    hn_m = _rmsnorm(meta_tokens, norm_mix, BF16)
    hn_p = jnp.concatenate([jnp.zeros((batch, pad, d), BF16), jnp.broadcast_to(hn_m[None], (batch, n_meta, d)),
                            hn_x.reshape(batch, seq, d)], axis=1).reshape(batch * lp, d)

    wb16 = w_in.astype(BF16)
    qkv_a = _proj(_proj_plain_body, hn_p, wb16[:, :c_z], BF16, _pick_tile(c_z, 1024, LANES), name="proj_qkv_a")
    zs = _proj(_proj_silu_body, hn_x, wb16[:, c_z:c_b], BF16, _pick_tile(aqk, 1024, LANES), name="proj_z")
    scale = LANES ** -0.5
    nw = jnp.concatenate([jnp.tile(q_norm_b * scale, hb), jnp.tile(k_norm_b, hb), jnp.ones((bw,), F32)]).reshape(1, 3 * bw)
    qkv_b = _proj(_proj_qkvb_body, hn_p, wb16[:, c_qkvb:c_f], BF16, bw, extra=(nw,), name="proj_qkv_b")
    gates = _proj(_proj_sigmoid_body, hn_x, wb16[:, c_g:], BF16, _pick_tile(2 * d, 1024, LANES), name="proj_gates")

    def rows8(w):
        return jnp.pad(w.T, ((0, 8 - w.shape[1]), (0, 0)))

    w_small = jnp.concatenate([rows8(wb16[:, c_b:c_a]), rows8(wb16[:, c_a:c_qkvb]), rows8(wb16[:, c_f:c_g]),
                               jnp.zeros((SMALL_ROWS - 24, d), BF16)], axis=0)
    small_t = _proj_small_t(hn_p, w_small)

    def lanes8(p):
        return jnp.broadcast_to(jnp.pad(p.astype(F32), (0, 8 - p.shape[0]))[:, None], (8, LANES))

    beta, gc, cumf = _prep(small_t, lanes8(a_log), lanes8(dt_bias), lanes8(f_bias), batch, lp, pad)
    n_chunks = lp // CHUNK
    beta5 = beta.reshape(batch, 8, n_chunks, 1, CHUNK)
    gc5 = gc.reshape(batch, 8, n_chunks, 1, CHUNK)
    u, w, qd, kt, at = _wy(qkv_a.reshape(batch, lp, 3 * aqk), conv_w, beta5, gc5, batch, lp, ha)
    o_a = _scan(u, w, qd, kt, at, gc5, zs.reshape(batch, seq, aqk), o_norm_a.reshape(1, LANES).astype(F32),
                batch, lp, ha)

    o_b = _fox(qkv_b.reshape(batch, lp, 3 * bw), cumf.reshape(batch * 8, 1, lp), batch, lp, hb, pad)

    res, xn = _merge(o_a.reshape(batch * seq, aqk), o_b.reshape(batch * seq, bw), gates, x2d,
                     w_branch[0].astype(BF16), w_branch[1].astype(BF16), w_out.astype(BF16),
                     norm_ffn.reshape(1, d).astype(F32))

    hp, n_keys = peer_sub_keys.shape[1], peer_sub_keys.shape[2]
    assert peer_sub_keys.shape[3] == LANES and n_keys == LANES and peer_wq.shape[1] == hp * 2 * LANES
    g_w, a_k, b_k = _route(xn, peer_wq.T.astype(BF16), peer_sub_keys.reshape(2 * hp * n_keys, LANES).astype(BF16),
                           hp, n_keys)
    out = _experts(xn, res, g_w, a_k, b_k, expert_u.T.astype(BF16), expert_v.astype(BF16), n_keys)
    return out.reshape(batch, seq, d)


def kernel(x, meta_tokens, norm_mix, w_in, conv_w, a_log, dt_bias, o_norm_a, q_norm_b, k_norm_b, f_bias,
           w_branch, w_out, norm_ffn, peer_wq, peer_sub_keys, expert_u, expert_v):
    assert norm_mix.shape[0] == 1, "single-layer block"
    return _layer(x, meta_tokens, norm_mix[0], w_in[0], conv_w[0], a_log[0], dt_bias[0], o_norm_a[0],
                  q_norm_b[0], k_norm_b[0], f_bias[0], w_branch[0], w_out[0], norm_ffn[0], peer_wq[0],
                  peer_sub_keys[0], expert_u[0], expert_v[0])
```

```python
import functools
import math

import jax
import jax.numpy as jnp
from jax import lax
from jax.experimental import pallas as pl
from jax.experimental.pallas import tpu as pltpu

F32, BF16, I32 = jnp.float32, jnp.bfloat16, jnp.int32
EPS = 1e-6
NEG_INF = -1e30
LANES = 128
BLOCK = 128
CHUNK = 64
CONV_K = 4
TOPK = 16
SMALL_ROWS = 32
VMEM_LIMIT = 56 * 1024 * 1024


def _cparams(sem):
    return pltpu.CompilerParams(dimension_semantics=sem, vmem_limit_bytes=VMEM_LIMIT)


def _pick_tile(n, target, mult):
    best = None
    for t in range(mult, min(n, target) + 1, mult):
        if n % t == 0:
            best = t
    assert best is not None, (n, target, mult)
    return best


def _nt_dot(a, b):
    return lax.dot_general(a, b, (((1,), (1,)), ((), ())), preferred_element_type=F32)


def _tn_dot(a, b):
    return lax.dot_general(a, b, (((0,), (0,)), ((), ())), preferred_element_type=F32)


def _split3(x):
    hi = x.astype(BF16)
    r = x - hi.astype(F32)
    mid = r.astype(BF16)
    lo = (r - mid.astype(F32)).astype(BF16)
    return hi, mid, lo


def _dot_01(x, m01):
    hi, mid, lo = _split3(x)
    d = functools.partial(jnp.dot, preferred_element_type=F32)
    return d(hi, m01) + (d(mid, m01) + d(lo, m01))


def _sigmoid(x):
    return 1.0 / (1.0 + jnp.exp(-x))


def _softplus(x):
    return jnp.maximum(x, 0.0) + jnp.log1p(jnp.exp(-jnp.abs(x)))


def _rmsnorm_body(x_ref, w_ref, o_ref):
    x = x_ref[...].astype(F32)
    y = x * lax.rsqrt(jnp.mean(x * x, axis=-1, keepdims=True) + EPS)
    o_ref[...] = (y * w_ref[...]).astype(o_ref.dtype)


def _rmsnorm(x2d, w, out_dtype):
    m, d = x2d.shape
    tm = _pick_tile(m, 1024, 8)
    return pl.pallas_call(
        _rmsnorm_body,
        grid=(m // tm,),
        in_specs=[pl.BlockSpec((tm, d), lambda i: (i, 0)), pl.BlockSpec((1, d), lambda i: (0, 0))],
        out_specs=pl.BlockSpec((tm, d), lambda i: (i, 0)),
        out_shape=jax.ShapeDtypeStruct((m, d), out_dtype),
        compiler_params=_cparams(("parallel",)),
        name="rmsnorm",
    )(x2d, w.reshape(1, d).astype(F32))


def _proj_plain_body(x_ref, w_ref, o_ref):
    o_ref[...] = jnp.dot(x_ref[...], w_ref[...], preferred_element_type=F32).astype(o_ref.dtype)


def _row_halves(x_ref, w_ref):
    th = x_ref.shape[0] // 2
    rows = [slice(0, th), slice(th, 2 * th)]
    return [(r, jnp.dot(x_ref[r, :], w_ref[...], preferred_element_type=F32)) for r in rows]


def _proj_silu_body(x_ref, w_ref, o_ref):
    for r, y in _row_halves(x_ref, w_ref):
        o_ref[r, :] = (y * _sigmoid(y)).astype(o_ref.dtype)


def _proj_sigmoid_body(x_ref, w_ref, o_ref):
    for r, y in _row_halves(x_ref, w_ref):
        o_ref[r, :] = _sigmoid(y).astype(o_ref.dtype)


def _proj_qkvb_body(x_ref, w_ref, nw_ref, o_ref):
    do_norm = pl.program_id(1) < 2
    for r, y in _row_halves(x_ref, w_ref):
        for h in range(y.shape[1] // LANES):
            sl = slice(h * LANES, (h + 1) * LANES)
            seg = y[:, sl]
            rs = lax.rsqrt(jnp.mean(seg * seg, axis=-1, keepdims=True) + EPS)
            rs = jnp.where(do_norm, rs, 1.0)
            o_ref[r, sl] = (seg * rs * nw_ref[:, sl]).astype(o_ref.dtype)


def _proj(body, x2d, w, out_dtype, tn, extra=(), name="proj"):
    m, k = x2d.shape
    n = w.shape[1]
    tm = _pick_tile(m, 1536, 128)
    assert n % tn == 0
    in_specs = [pl.BlockSpec((tm, k), lambda i, j: (i, 0)), pl.BlockSpec((k, tn), lambda i, j: (0, j))]
    in_specs += [pl.BlockSpec((1, tn), lambda i, j: (0, j)) for _ in extra]
    return pl.pallas_call(
        body,
        grid=(m // tm, n // tn),
        in_specs=in_specs,
        out_specs=pl.BlockSpec((tm, tn), lambda i, j: (i, j)),
        out_shape=jax.ShapeDtypeStruct((m, n), out_dtype),
        compiler_params=_cparams(("parallel", "arbitrary")),
        name=name,
    )(x2d, w, *extra)


def _small_body(w_ref, x_ref, o_ref):
    o_ref[...] = _nt_dot(w_ref[...], x_ref[...])


def _proj_small_t(x2d, w_t):
    m, k = x2d.shape
    tm = _pick_tile(m, 2048, 128)
    return pl.pallas_call(
        _small_body,
        grid=(m // tm,),
        in_specs=[pl.BlockSpec((SMALL_ROWS, k), lambda i: (0, 0)), pl.BlockSpec((tm, k), lambda i: (i, 0))],
        out_specs=pl.BlockSpec((SMALL_ROWS, tm), lambda i: (0, i)),
        out_shape=jax.ShapeDtypeStruct((SMALL_ROWS, m), F32),
        compiler_params=_cparams(("parallel",)),
        name="proj_small",
    )(w_t, x2d)


def _prep_body(s_ref, alog_ref, dtb_ref, fb_ref, beta_ref, gc_ref, cumf_ref, carry_ref, *, pad):
    j = pl.program_id(1)

    @pl.when(j == 0)
    def _():
        carry_ref[...] = jnp.zeros_like(carry_ref)

    tl = s_ref.shape[1]
    sm = s_ref[...]
    b, a, f = sm[0:8], sm[8:16], sm[16:24]
    pos = j * tl + lax.broadcasted_iota(I32, (8, tl), 1)
    valid = pos >= pad
    bc = lambda p_ref: p_ref[:, 0:1]
    beta = jnp.where(valid, _sigmoid(b), 0.0)
    g = jnp.where(valid, -jnp.exp(bc(alog_ref)) * _softplus(a + bc(dtb_ref)), 0.0)
    logf = jnp.where(valid, -_softplus(-(f + bc(fb_ref))), 0.0)
    r = lax.broadcasted_iota(I32, (tl, tl), 0)
    c = lax.broadcasted_iota(I32, (tl, tl), 1)
    upper = r <= c
    shift = CHUNK.bit_length() - 1
    same_chunk = lax.shift_right_logical(r, shift) == lax.shift_right_logical(c, shift)
    m_seq = jnp.where(upper, 1.0, 0.0).astype(BF16)
    m_chunk = jnp.where(upper & same_chunk, 1.0, 0.0).astype(BF16)
    beta_ref[0] = beta
    gc_ref[0] = _dot_01(g, m_chunk)
    cum = _dot_01(logf, m_seq) + carry_ref[:, 0:1]
    cumf_ref[0] = cum
    carry_ref[...] = jnp.broadcast_to(cum[:, tl - 1:tl], (8, LANES))


def _prep(small_t, alog, dtb, fb, batch, lp, pad):
    tl = _pick_tile(lp, 512, LANES)
    nb = lp // tl
    out = jax.ShapeDtypeStruct((batch, 8, lp), F32)
    pspec = pl.BlockSpec((8, LANES), lambda b, j: (0, 0))
    ospec = pl.BlockSpec((1, 8, tl), lambda b, j: (b, 0, j))
    return pl.pallas_call(
        functools.partial(_prep_body, pad=pad),
        grid=(batch, nb),
        in_specs=[pl.BlockSpec((SMALL_ROWS, tl), lambda b, j: (0, b * nb + j)), pspec, pspec, pspec],
        out_specs=[ospec, ospec, ospec],
        out_shape=[out, out, out],
        scratch_shapes=[pltpu.VMEM((8, LANES), F32)],
        compiler_params=_cparams(("parallel", "arbitrary")),
        name="gate_prep",
    )(small_t, alog, dtb, fb)


SUB = 16
HALO_ROWS = 16


def _bmm(a, b):
    return jnp.einsum('cij,cjk->cik', a.astype(BF16), b.astype(BF16), preferred_element_type=F32)


def _bmm_nt(a, b):
    return jnp.einsum('cik,cjk->cij', a.astype(BF16), b.astype(BF16), preferred_element_type=F32)


def _wy_body(x_ref, halo_ref, cw_ref, beta_ref, gc_ref, u_ref, w_ref, qd_ref, kt_ref, at_ref, *, heads, cb):
    rows = cb * CHUNK
    aqk = heads * LANES
    halo = halo_ref[0].astype(F32)[HALO_ROWS - 8:]
    xw = jnp.concatenate([halo, x_ref[0].astype(F32)], axis=0)
    cw = cw_ref[...]
    y = cw[0:1] * xw[5:5 + rows]
    for i in range(1, CONV_K):
        y = y + cw[i:i + 1] * xw[5 + i:5 + i + rows]
    y = y * _sigmoid(y)
    ii = lax.broadcasted_iota(I32, (CHUNK, CHUNK), 0)
    jj = lax.broadcasted_iota(I32, (CHUNK, CHUNK), 1)
    eye = ii == jj
    incl = ii >= jj
    sub_shift = SUB.bit_length() - 1
    same_sub = lax.shift_right_logical(ii, sub_shift) == lax.shift_right_logical(jj, sub_shift)
    eye_f = jnp.where(eye, 1.0, 0.0)

    def heads_batched(off):
        return jnp.concatenate([y[:, off + h * LANES:off + (h + 1) * LANES].reshape(cb, CHUNK, LANES)
                                for h in range(heads)], axis=0)

    q, k, v = heads_batched(0), heads_batched(aqk), heads_batched(2 * aqk)
    q = q * lax.rsqrt(jnp.sum(q * q, axis=-1, keepdims=True) + EPS) * (LANES ** -0.5)
    k = k * lax.rsqrt(jnp.sum(k * k, axis=-1, keepdims=True) + EPS)
    b_row = jnp.concatenate([beta_ref[0, h] for h in range(heads)], axis=0)
    g_row = jnp.concatenate([gc_ref[0, h] for h in range(heads)], axis=0)
    b_col = jnp.sum(jnp.where(eye, b_row, 0.0), axis=2, keepdims=True)
    g_col = jnp.sum(jnp.where(eye, g_row, 0.0), axis=2, keepdims=True)
    gamma = jnp.where(incl, jnp.exp(jnp.where(incl, g_col - g_row, 0.0)), 0.0)
    kb = k * b_col
    a_mat = jnp.where(ii > jj, _bmm_nt(kb, k) * gamma, 0.0)
    attn = _bmm_nt(q, k) * gamma
    x = jnp.where(same_sub, -a_mat, 0.0)
    t_d = eye_f + x
    for _ in range(3):
        x = _bmm(x, x)
        t_d = t_d + _bmm(t_d, x)
    m1 = _bmm(t_d, jnp.where(same_sub, 0.0, a_mat))
    m2 = _bmm(m1, m1)
    t = _bmm(eye_f - m1 + m2 - _bmm(m1, m2), t_d)
    e_g = jnp.exp(g_col)
    sol = _bmm(t, jnp.concatenate([v * b_col, kb * e_g], axis=2))
    g_last = g_row[:, :, CHUNK - 1:CHUNK]
    qd = q * e_g
    kt = k * jnp.exp(g_last - g_col)
    at = jnp.concatenate([attn, jnp.zeros_like(attn)], axis=2)
    for h in range(heads):
        hs = slice(h * LANES, (h + 1) * LANES)
        bs = slice(h * cb, (h + 1) * cb)
        u_ref[0, :, hs] = sol[bs, :, :LANES].reshape(rows, LANES)
        w_ref[0, :, hs] = sol[bs, :, LANES:].reshape(rows, LANES).astype(BF16)
        qd_ref[0, :, hs] = qd[bs].reshape(rows, LANES).astype(BF16)
        kt_ref[0, :, hs] = kt[bs].reshape(rows, LANES).astype(BF16)
        at_ref[0, :, hs] = at[bs].reshape(rows, LANES).astype(BF16)


def _wy(qkv_a, conv_w, beta5, gc5, batch, lp, heads):
    n_chunks = lp // CHUNK
    cb = max(c for c in range(1, 7) if n_chunks % c == 0)
    rows = cb * CHUNK
    width = heads * LANES
    ospec = pl.BlockSpec((1, rows, width), lambda b, i: (b, i, 0))
    rspec = pl.BlockSpec((1, 8, cb, 1, CHUNK), lambda b, i: (b, 0, i, 0, 0))
    f32o = jax.ShapeDtypeStruct((batch, lp, width), F32)
    b16o = jax.ShapeDtypeStruct((batch, lp, width), BF16)
    return pl.pallas_call(
        functools.partial(_wy_body, heads=heads, cb=cb),
        grid=(batch, n_chunks // cb),
        in_specs=[pl.BlockSpec((1, rows, 3 * width), lambda b, i: (b, i, 0)),
                  pl.BlockSpec((1, HALO_ROWS, 3 * width),
                               lambda b, i: (b, jnp.maximum(i * (rows // HALO_ROWS) - 1, 0), 0)),
                  pl.BlockSpec((CONV_K, 3 * width), lambda b, i: (0, 0)), rspec, rspec],
        out_specs=[ospec] * 5,
        out_shape=[f32o, b16o, b16o, b16o, b16o],
        compiler_params=_cparams(("parallel", "parallel")),
        name="delta_rule_wy",
    )(qkv_a, qkv_a, conv_w, beta5, gc5)


def _scan_body(u_ref, w_ref, qd_ref, kt_ref, at_ref, gc_ref, zs_ref, onw_ref, o_ref, s_s, *, heads):
    @pl.when(pl.program_id(1) == 0)
    def _():
        s_s[...] = jnp.zeros_like(s_s)

    hsl = [slice(h * LANES, (h + 1) * LANES) for h in range(heads)]
    for c in range(BLOCK // CHUNK):
        rs = slice(c * CHUNK, (c + 1) * CHUNK)
        s = [s_s[h] for h in range(heads)]
        ws = [jnp.dot(jnp.concatenate([w_ref[0, rs, hs], qd_ref[0, rs, hs]], axis=0), s[h].astype(BF16),
                      preferred_element_type=F32) for h, hs in enumerate(hsl)]
        vb = [(u_ref[0, rs, hs] - ws[h][:CHUNK]).astype(BF16) for h, hs in enumerate(hsl)]
        upd = [_tn_dot(kt_ref[0, rs, hs], vb[h]) for h, hs in enumerate(hsl)]
        o = [ws[h][CHUNK:] + jnp.dot(at_ref[0, rs, h * LANES:h * LANES + CHUNK], vb[h], preferred_element_type=F32)
             for h in range(heads)]
        for h, hs in enumerate(hsl):
            dec = jnp.exp(gc_ref[0, h, c][:, CHUNK - 1:CHUNK])
            s_s[h] = s[h] * dec + upd[h]
            on = o[h] * lax.rsqrt(jnp.mean(o[h] * o[h], axis=-1, keepdims=True) + EPS) * onw_ref[...]
            o_ref[0, rs, hs] = (on * zs_ref[0, rs, hs].astype(F32)).astype(o_ref.dtype)


def _scan(u, w, qd, kt, at, gc5, zs, onw, batch, lp, heads):
    seq = lp - BLOCK
    width = heads * LANES
    ispec = pl.BlockSpec((1, BLOCK, width), lambda b, i: (b, i, 0))
    xspec = pl.BlockSpec((1, BLOCK, width), lambda b, i: (b, jnp.maximum(i - 1, 0), 0))
    return pl.pallas_call(
        functools.partial(_scan_body, heads=heads),
        grid=(batch, lp // BLOCK),
        in_specs=[ispec] * 5 + [pl.BlockSpec((1, 8, BLOCK // CHUNK, 1, CHUNK), lambda b, i: (b, 0, i, 0, 0)),
                                xspec, pl.BlockSpec((1, LANES), lambda b, i: (0, 0))],
        out_specs=xspec,
        out_shape=jax.ShapeDtypeStruct((batch, seq, width), BF16),
        scratch_shapes=[pltpu.VMEM((heads, LANES, LANES), F32)],
        compiler_params=_cparams(("parallel", "arbitrary")),
        name="delta_rule_scan",
    )(u, w, qd, kt, at, gc5, zs, onw)


def _fox_body(q_ref, k_ref, v_ref, cum_ref, o_ref, *, tq, pad):
    i = pl.program_id(2)
    q0 = pl.multiple_of(BLOCK + i * tq, BLOCK)
    th = tq // 2
    q = q_ref[0, pl.ds(q0, tq), :]

    def tile(qx, k0, size, mask, carry):
        m, l, acc = carry
        s = _nt_dot(qx, k_ref[0, pl.ds(k0, size), :]) - cum_ref[0, :, pl.ds(k0, size)]
        if mask is not None:
            s = jnp.where(mask, s, NEG_INF)
        m_new = jnp.maximum(m, jnp.max(s, axis=1, keepdims=True))
        alpha = jnp.exp(m - m_new)
        p = jnp.exp(s - m_new)
        l = alpha * l + jnp.sum(p, axis=1, keepdims=True)
        acc = alpha * acc + jnp.dot(p.astype(BF16), v_ref[0, pl.ds(k0, size), :], preferred_element_type=F32)
        return m_new, l, acc

    carry = (jnp.full((tq, 1), NEG_INF, F32), jnp.zeros((tq, 1), F32), jnp.zeros((tq, LANES), F32))
    carry = tile(q, 0, BLOCK, lax.broadcasted_iota(I32, (tq, BLOCK), 1) >= pad, carry)
    carry = lax.fori_loop(0, i, lambda j, c: tile(q, pl.multiple_of(BLOCK + j * tq, BLOCK), tq, None, c), carry)
    m, l, acc = tile(q, q0, th, lax.broadcasted_iota(I32, (tq, th), 1) <= lax.broadcasted_iota(I32, (tq, th), 0), carry)
    o_ref[0, :th, :] = (acc[:th] / l[:th]).astype(o_ref.dtype)
    causal = lax.broadcasted_iota(I32, (th, th), 1) <= lax.broadcasted_iota(I32, (th, th), 0)
    m, l, acc = tile(q[th:], q0 + th, th, causal, (m[th:], l[th:], acc[th:]))
    o_ref[0, th:, :] = (acc / l).astype(o_ref.dtype)


def _fox(qkv, cumf3, batch, lp, heads, pad):
    seq = lp - BLOCK
    tq = _pick_tile(seq, 1024, 2 * BLOCK)
    kvspec = lambda off: pl.BlockSpec((1, lp, LANES), lambda b, h, i: (b, 0, off + h))
    return pl.pallas_call(
        functools.partial(_fox_body, tq=tq, pad=pad),
        grid=(batch, heads, seq // tq),
        in_specs=[kvspec(0), kvspec(heads), kvspec(2 * heads),
                  pl.BlockSpec((1, 1, lp), lambda b, h, i: (b * 8 + h, 0, 0))],
        out_specs=pl.BlockSpec((1, tq, LANES), lambda b, h, i: (b, i, h)),
        out_shape=jax.ShapeDtypeStruct((batch, seq, heads * LANES), BF16),
        compiler_params=_cparams(("parallel", "parallel", "arbitrary")),
        name="forgetting_attention",
    )(qkv, qkv, qkv, cumf3)


def _merge_body(oa_ref, ob_ref, g_ref, x_ref, wa_ref, wb_ref, wo_ref, nw_ref, res_ref, xn_ref):
    d = x_ref.shape[1]
    ya = jnp.dot(oa_ref[...], wa_ref[...], preferred_element_type=F32)
    yb = jnp.dot(ob_ref[...], wb_ref[...], preferred_element_type=F32)
    g = g_ref[...].astype(F32)
    mix = g[:, :d] * ya + g[:, d:] * yb
    res = x_ref[...] + jnp.dot(mix.astype(BF16), wo_ref[...], preferred_element_type=F32)
    res_ref[...] = res
    xn = res * lax.rsqrt(jnp.mean(res * res, axis=-1, keepdims=True) + EPS) * nw_ref[...]
    xn_ref[...] = xn.astype(xn_ref.dtype)


def _merge(oa, ob, gates, x2d, wa, wb, wo, nw):
    m, d = x2d.shape
    bw = oa.shape[1]
    tm = _pick_tile(m, 512, 128)
    row = lambda c: pl.BlockSpec((tm, c), lambda i: (i, 0))
    full = lambda r, c: pl.BlockSpec((r, c), lambda i: (0, 0))
    return pl.pallas_call(
        _merge_body,
        grid=(m // tm,),
        in_specs=[row(bw), row(bw), row(2 * d), row(d), full(bw, d), full(bw, d), full(d, d), full(1, d)],
        out_specs=[row(d), row(d)],
        out_shape=[jax.ShapeDtypeStruct((m, d), F32), jax.ShapeDtypeStruct((m, d), BF16)],
        compiler_params=_cparams(("parallel",)),
        name="merge_out_proj",
    )(oa, ob, gates, x2d, wa, wb, wo, nw)


_CAND_GROUPS = ((0, 0, 8, 8), (0, 8, 16, 8), (1, 0, 8, 8), (2, 0, 8, 5), (3, 0, 8, 4), (4, 0, 8, 3),
                (5, 0, 8, 2), (6, 0, 8, 2), (7, 0, 8, 2))


def _sort_network(n):
    pairs, p = [], 1
    while p < n:
        k = p
        while k >= 1:
            for j in range(k % p, n - k, 2 * k):
                for i in range(min(k, n - j - k)):
                    if (i + j) // (2 * p) == (i + j + k) // (2 * p):
                        pairs.append((i + j, i + j + k))
            k //= 2
        p *= 2
    return pairs


def _better(a, b):
    return (a[0] > b[0]) | ((a[0] == b[0]) & (a[1] < b[1]))


def _top_rows(groups, n, out_refs):
    depth = len(groups)
    size = 1
    while size < depth:
        size *= 2
    g = list(groups) + [None] * (size - depth)
    for i, j in _sort_network(size):
        if g[j] is None:
            continue
        if g[i] is None:
            g[i], g[j] = g[j], None
            continue
        take = _better(g[j], g[i])
        g[i], g[j] = (tuple(jnp.where(take, y, x) for x, y in zip(g[i], g[j])),
                      tuple(jnp.where(take, x, y) for x, y in zip(g[i], g[j])))
    g = g[:depth]
    worst = (-jnp.inf, 1e9)
    for kk in range(n):
        best = g[0]
        for shift in (4, 2, 1):
            other = tuple(pltpu.roll(x, shift, 0) for x in best)
            take = _better(other, best)
            best = tuple(jnp.where(take, y, x) for x, y in zip(best, other))
        for ref, x in zip(out_refs, best):
            ref[kk:kk + 1, :] = x[0:1]
        win = g[0][1] == best[1]
        for d in range(min(depth, n - kk - 1)):
            nxt = g[d + 1] if d + 1 < depth else tuple(jnp.full_like(x, w) for x, w in zip(g[d], worst + (0.0,) * 8))
            g[d] = tuple(jnp.where(win, y, x) for x, y in zip(g[d], nxt))


def _route_body(xn_ref, wqt_ref, keys_ref, g_ref, a_ref, b_ref, qt_s, gt_s, at_s, bt_s, v_s, i_s, sc_s, fl_s, ea_s,
                eb_s, *, heads, n_keys):
    tm = xn_ref.shape[0]
    qt_s[...] = _nt_dot(wqt_ref[...], xn_ref[...]).astype(BF16)

    def row_ids(shape):
        return lax.broadcasted_iota(I32, shape, 0) + lax.shift_right_logical(lax.broadcasted_iota(I32, shape, 1), 31)

    key_iota = row_ids((n_keys, LANES)).astype(F32)
    sub = row_ids((8, LANES))

    def head(h, carry):
        st = []
        for p in range(2):
            qhp = qt_s[pl.ds(pl.multiple_of(h * 2 * LANES + p * LANES, LANES), LANES), :]
            kk = keys_ref[pl.ds(pl.multiple_of((p * heads + h) * n_keys, n_keys), n_keys), :]
            st.append(jnp.dot(kk, qhp, preferred_element_type=F32))
        for c in range(tm // LANES):
            cs = slice(c * LANES, (c + 1) * LANES)
            for p in range(2):
                ps = slice(p * tm + c * LANES, p * tm + (c + 1) * LANES)
                _top_rows([(st[p][8 * d:8 * d + 8, cs], key_iota[8 * d:8 * d + 8]) for d in range(n_keys // 8)], TOPK,
                          (v_s.at[:, ps], i_s.at[:, ps]))
        for c in range(tm // LANES):
            cs = slice(c * LANES, (c + 1) * LANES)
            v1, v2 = v_s[:, cs], v_s[:, tm + c * LANES:tm + (c + 1) * LANES]
            i1, i2 = i_s[:, cs], i_s[:, tm + c * LANES:tm + (c + 1) * LANES]
            groups = []
            for (a, b0, b1, nvalid) in _CAND_GROUPS:
                groups.append((jnp.where(sub < nvalid, v1[a:a + 1] + v2[b0:b1], -jnp.inf),
                               (a * TOPK + b0 + sub).astype(F32), jnp.broadcast_to(i1[a:a + 1], (8, LANES)), i2[b0:b1]))
            groups.append((v1[8:16] + v2[0:1], ((8 + sub) * TOPK).astype(F32), i1[8:16],
                           jnp.broadcast_to(i2[0:1], (8, LANES))))
            _top_rows(groups, TOPK, (sc_s.at[:, cs], fl_s.at[:, cs], ea_s.at[:, cs], eb_s.at[:, cs]))
        sc = sc_s[...]
        e = jnp.exp(sc - sc[0:1])
        row = pl.ds(pl.multiple_of(h * TOPK, TOPK), TOPK)
        gt_s[row, :] = e / jnp.sum(e, axis=0, keepdims=True)
        at_s[row, :] = ea_s[...]
        bt_s[row, :] = eb_s[...]
        return carry

    lax.fori_loop(0, heads, head, 0)
    g_ref[...] = gt_s[...].T
    a_ref[...] = at_s[...].T
    b_ref[...] = bt_s[...].T


def _route(xn, wq_t, keys2d, heads, n_keys):
    t, d = xn.shape
    tm = _pick_tile(t, 256, LANES)
    hk = heads * TOPK
    out = jax.ShapeDtypeStruct((t, hk), F32)
    ospec = pl.BlockSpec((tm, hk), lambda i: (i, 0))
    tmp = lambda w: pltpu.VMEM((TOPK, w), F32)
    return pl.pallas_call(
        functools.partial(_route_body, heads=heads, n_keys=n_keys),
        grid=(t // tm,),
        in_specs=[pl.BlockSpec((tm, d), lambda i: (i, 0)),
                  pl.BlockSpec(wq_t.shape, lambda i: (0, 0)),
                  pl.BlockSpec(keys2d.shape, lambda i: (0, 0))],
        out_specs=[ospec, ospec, ospec],
        out_shape=[out, out, out],
        scratch_shapes=[pltpu.VMEM((wq_t.shape[0], tm), BF16)] + [pltpu.VMEM((hk, tm), F32) for _ in range(3)]
        + [tmp(2 * tm), tmp(2 * tm), tmp(tm), tmp(tm), tmp(tm), tmp(tm)],
        compiler_params=_cparams(("parallel",)),
        name="peer_route",
    )(xn, wq_t, keys2d)


def _gelu(x):
    return 0.5 * x * (1.0 + lax.erf(x * (2.0 ** -0.5)))


W_PITCH_PAD = 8
MXU_DIM = 256


def _unpack_bf16_row(w, odd):
    bits = (w & jnp.uint32(0xFFFF0000)) if odd else lax.shift_left(w, jnp.uint32(16))
    return lax.bitcast_convert_type(bits, F32)


def _expert_body(xn_ref, res_ref, g_ref, a_ref, b_ref, ut_ref, v_ref, o_ref, w_s, acc_s, *, n_keys):
    j = pl.program_id(1)
    tm = xn_ref.shape[0]
    te = ut_ref.shape[1]
    pitch = n_keys // 2 + W_PITCH_PAD

    @pl.when(j == 0)
    def _():
        acc_s[...] = jnp.zeros_like(acc_s)
        key = lax.broadcasted_iota(I32, (n_keys, g_ref.shape[1]), 0).astype(BF16)

        def token(t, carry):
            a_row = a_ref[pl.ds(t, 1), :].astype(BF16)
            b_row = b_ref[pl.ds(t, 1), :].astype(BF16)
            g_row = g_ref[pl.ds(t, 1), :].astype(BF16)
            pt = jnp.where(key == a_row, g_row, jnp.zeros_like(g_row))
            qt = jnp.where(key == b_row, jnp.ones_like(g_row), jnp.zeros_like(g_row))
            w_t = _nt_dot(pt, qt).astype(BF16)
            w_s[pl.ds(pl.multiple_of(t * pitch, 8), n_keys // 2), :] = pltpu.bitcast(w_t, jnp.uint32)
            return carry

        lax.fori_loop(0, tm, token, 0, unroll=64)

    n_slab = te // n_keys
    act = jnp.dot(xn_ref[...], ut_ref[...], preferred_element_type=F32)
    coef = []
    for s in range(0, n_slab, 2):
        packed = w_s[pl.ds(j * (n_slab // 2) + s // 2, tm, stride=pitch), :]
        for odd in range(2):
            sl = slice((s + odd) * n_keys, (s + odd + 1) * n_keys)
            coef.append((_gelu(act[:, sl]) * _unpack_bf16_row(packed, odd)).astype(BF16))
    acc_s[...] += jnp.dot(jnp.concatenate(coef, axis=1), v_ref[...], preferred_element_type=F32)

    @pl.when(j == pl.num_programs(1) - 1)
    def _():
        o_ref[...] = res_ref[...] + acc_s[...]


def _experts(xn, res, g, a, b, u_t, v, n_keys):
    t, d = xn.shape
    ne = v.shape[0]
    hk = g.shape[1]
    tm = _pick_tile(t, 512, 2 * LANES)
    te = _pick_tile(ne, 2048, MXU_DIM)
    tok = lambda c: pl.BlockSpec((tm, c), lambda i, j: (i, 0))
    return pl.pallas_call(
        functools.partial(_expert_body, n_keys=n_keys),
        grid=(t // tm, ne // te),
        in_specs=[tok(d), tok(d), tok(hk), tok(hk), tok(hk),
                  pl.BlockSpec((d, te), lambda i, j: (0, j)), pl.BlockSpec((te, d), lambda i, j: (j, 0))],
        out_specs=tok(d),
        out_shape=jax.ShapeDtypeStruct((t, d), F32),
        scratch_shapes=[pltpu.VMEM(((n_keys // 2 + W_PITCH_PAD) * tm, n_keys), jnp.uint32),
                        pltpu.VMEM((tm, d), F32)],
        compiler_params=_cparams(("parallel", "arbitrary")),
        name="peer_experts",
    )(xn, res, g, a, b, u_t, v)


def _layer(x, meta_tokens, norm_mix, w_in, conv_w, a_log, dt_bias, o_norm_a, q_norm_b, k_norm_b, f_bias,
           w_branch, w_out, norm_ffn, peer_wq, peer_sub_keys, expert_u, expert_v):
    batch, seq, d = x.shape
    n_meta = meta_tokens.shape[0]
    pad = BLOCK - n_meta
    lp = BLOCK + seq
    ha, hb = a_log.shape[0], f_bias.shape[0]
    aqk = ha * LANES
    bw = hb * LANES
    assert ha <= 8 and hb <= 8 and seq % BLOCK == 0 and conv_w.shape == (CONV_K, 3 * aqk)
    assert w_in.shape[1] == 4 * aqk + 2 * ha + 3 * bw + hb + 2 * d
    c_z, c_b, c_a = 3 * aqk, 4 * aqk, 4 * aqk + ha
    c_qkvb = 4 * aqk + 2 * ha
    c_f = c_qkvb + 3 * bw
    c_g = c_f + hb

    x2d = x.reshape(batch * seq, d)
    hn_x = _rmsnorm(x2d, norm_mix, BF16)
    hn_m = _rmsnorm(meta_tokens, norm_mix, BF16)
    hn_p = jnp.concatenate([jnp.zeros((batch, pad, d), BF16), jnp.broadcast_to(hn_m[None], (batch, n_meta, d)),
                            hn_x.reshape(batch, seq, d)], axis=1).reshape(batch * lp, d)

    wb16 = w_in.astype(BF16)
    qkv_a = _proj(_proj_plain_body, hn_p, wb16[:, :c_z], BF16, _pick_tile(c_z, 1024, LANES), name="proj_qkv_a")
    zs = _proj(_proj_silu_body, hn_x, wb16[:, c_z:c_b], BF16, _pick_tile(aqk, 1024, LANES), name="proj_z")
    scale = LANES ** -0.5
    nw = jnp.concatenate([jnp.tile(q_norm_b * scale, hb), jnp.tile(k_norm_b, hb), jnp.ones((bw,), F32)]).reshape(1, 3 * bw)
    qkv_b = _proj(_proj_qkvb_body, hn_p, wb16[:, c_qkvb:c_f], BF16, bw, extra=(nw,), name="proj_qkv_b")
    gates = _proj(_proj_sigmoid_body, hn_x, wb16[:, c_g:], BF16, _pick_tile(2 * d, 1024, LANES), name="proj_gates")

    def rows8(w):
        return jnp.pad(w.T, ((0, 8 - w.shape[1]), (0, 0)))

    w_small = jnp.concatenate([rows8(wb16[:, c_b:c_a]), rows8(wb16[:, c_a:c_qkvb]), rows8(wb16[:, c_f:c_g]),
                               jnp.zeros((SMALL_ROWS - 24, d), BF16)], axis=0)
    small_t = _proj_small_t(hn_p, w_small)

    def lanes8(p):
        return jnp.broadcast_to(jnp.pad(p.astype(F32), (0, 8 - p.shape[0]))[:, None], (8, LANES))

    beta, gc, cumf = _prep(small_t, lanes8(a_log), lanes8(dt_bias), lanes8(f_bias), batch, lp, pad)
    n_chunks = lp // CHUNK
    beta5 = beta.reshape(batch, 8, n_chunks, 1, CHUNK)
    gc5 = gc.reshape(batch, 8, n_chunks, 1, CHUNK)
    u, w, qd, kt, at = _wy(qkv_a.reshape(batch, lp, 3 * aqk), conv_w, beta5, gc5, batch, lp, ha)
    o_a = _scan(u, w, qd, kt, at, gc5, zs.reshape(batch, seq, aqk), o_norm_a.reshape(1, LANES).astype(F32),
                batch, lp, ha)

    o_b = _fox(qkv_b.reshape(batch, lp, 3 * bw), cumf.reshape(batch * 8, 1, lp), batch, lp, hb, pad)

    res, xn = _merge(o_a.reshape(batch * seq, aqk), o_b.reshape(batch * seq, bw), gates, x2d,
                     w_branch[0].astype(BF16), w_branch[1].astype(BF16), w_out.astype(BF16),
                     norm_ffn.reshape(1, d).astype(F32))

    hp, n_keys = peer_sub_keys.shape[1], peer_sub_keys.shape[2]
    assert peer_sub_keys.shape[3] == LANES and n_keys == LANES and peer_wq.shape[1] == hp * 2 * LANES
    g_w, a_k, b_k = _route(xn, peer_wq.T.astype(BF16), peer_sub_keys.reshape(2 * hp * n_keys, LANES).astype(BF16),
                           hp, n_keys)
    out = _experts(xn, res, g_w, a_k, b_k, expert_u.T.astype(BF16), expert_v.astype(BF16), n_keys)
    return out.reshape(batch, seq, d)


def kernel(x, meta_tokens, norm_mix, w_in, conv_w, a_log, dt_bias, o_norm_a, q_norm_b, k_norm_b, f_bias,
           w_branch, w_out, norm_ffn, peer_wq, peer_sub_keys, expert_u, expert_v):
    assert norm_mix.shape[0] == 1, "single-layer block"
    return _layer(x, meta_tokens, norm_mix[0], w_in[0], conv_w[0], a_log[0], dt_bias[0], o_norm_a[0],
                  q_norm_b[0], k_norm_b[0], f_bias[0], w_branch[0], w_out[0], norm_ffn[0], peer_wq[0],
                  peer_sub_keys[0], expert_u[0], expert_v[0])
```

```python
import functools
import math

import jax
import jax.numpy as jnp
from jax import lax
from jax.experimental import pallas as pl
from jax.experimental.pallas import tpu as pltpu

F32, BF16, I32 = jnp.float32, jnp.bfloat16, jnp.int32
EPS = 1e-6
NEG_INF = -1e30
LOG2E = math.log2(math.e)
LANES = 128
BLOCK = 128
CHUNK = 64
CONV_K = 4
TOPK = 16
SMALL_ROWS = 32
VMEM_LIMIT = 56 * 1024 * 1024


def _cparams(sem):
    return pltpu.CompilerParams(dimension_semantics=sem, vmem_limit_bytes=VMEM_LIMIT)


def _pick_tile(n, target, mult):
    best = None
    for t in range(mult, min(n, target) + 1, mult):
        if n % t == 0:
            best = t
    assert best is not None, (n, target, mult)
    return best


def _nt_dot(a, b):
    return lax.dot_general(a, b, (((1,), (1,)), ((), ())), preferred_element_type=F32)


def _tn_dot(a, b):
    return lax.dot_general(a, b, (((0,), (0,)), ((), ())), preferred_element_type=F32)


def _split3(x):
    hi = x.astype(BF16)
    r = x - hi.astype(F32)
    mid = r.astype(BF16)
    lo = (r - mid.astype(F32)).astype(BF16)
    return hi, mid, lo


def _dot_01(x, m01):
    hi, mid, lo = _split3(x)
    d = functools.partial(jnp.dot, preferred_element_type=F32)
    return d(hi, m01) + (d(mid, m01) + d(lo, m01))


def _sigmoid(x):
    return 1.0 / (1.0 + jnp.exp(-x))


def _softplus(x):
    return jnp.maximum(x, 0.0) + jnp.log1p(jnp.exp(-jnp.abs(x)))


def _rmsnorm_body(x_ref, w_ref, o_ref):
    x = x_ref[...].astype(F32)
    y = x * lax.rsqrt(jnp.mean(x * x, axis=-1, keepdims=True) + EPS)
    o_ref[...] = (y * w_ref[...]).astype(o_ref.dtype)


def _rmsnorm(x2d, w, out_dtype):
    m, d = x2d.shape
    tm = _pick_tile(m, 1024, 8)
    return pl.pallas_call(
        _rmsnorm_body,
        grid=(m // tm,),
        in_specs=[pl.BlockSpec((tm, d), lambda i: (i, 0)), pl.BlockSpec((1, d), lambda i: (0, 0))],
        out_specs=pl.BlockSpec((tm, d), lambda i: (i, 0)),
        out_shape=jax.ShapeDtypeStruct((m, d), out_dtype),
        compiler_params=_cparams(("parallel",)),
        name="rmsnorm",
    )(x2d, w.reshape(1, d).astype(F32))


def _proj_plain_body(x_ref, w_ref, o_ref):
    o_ref[...] = jnp.dot(x_ref[...], w_ref[...], preferred_element_type=F32).astype(o_ref.dtype)


def _row_halves(x_ref, w_ref):
    th = x_ref.shape[0] // 2
    rows = [slice(0, th), slice(th, 2 * th)]
    return [(r, jnp.dot(x_ref[r, :], w_ref[...], preferred_element_type=F32)) for r in rows]


def _proj_silu_body(x_ref, w_ref, o_ref):
    for r, y in _row_halves(x_ref, w_ref):
        o_ref[r, :] = (y * _sigmoid(y)).astype(o_ref.dtype)


def _proj_sigmoid_body(x_ref, w_ref, o_ref):
    for r, y in _row_halves(x_ref, w_ref):
        o_ref[r, :] = _sigmoid(y).astype(o_ref.dtype)


def _proj_qkvb_body(x_ref, w_ref, nw_ref, o_ref):
    do_norm = pl.program_id(1) < 2
    for r, y in _row_halves(x_ref, w_ref):
        for h in range(y.shape[1] // LANES):
            sl = slice(h * LANES, (h + 1) * LANES)
            seg = y[:, sl]
            rs = lax.rsqrt(jnp.mean(seg * seg, axis=-1, keepdims=True) + EPS)
            rs = jnp.where(do_norm, rs, 1.0)
            o_ref[r, sl] = (seg * rs * nw_ref[:, sl]).astype(o_ref.dtype)


def _proj(body, x2d, w, out_dtype, tn, extra=(), name="proj"):
    m, k = x2d.shape
    n = w.shape[1]
    tm = _pick_tile(m, 1536, 128)
    assert n % tn == 0
    in_specs = [pl.BlockSpec((tm, k), lambda i, j: (i, 0)), pl.BlockSpec((k, tn), lambda i, j: (0, j))]
    in_specs += [pl.BlockSpec((1, tn), lambda i, j: (0, j)) for _ in extra]
    return pl.pallas_call(
        body,
        grid=(m // tm, n // tn),
        in_specs=in_specs,
        out_specs=pl.BlockSpec((tm, tn), lambda i, j: (i, j)),
        out_shape=jax.ShapeDtypeStruct((m, n), out_dtype),
        compiler_params=_cparams(("parallel", "arbitrary")),
        name=name,
    )(x2d, w, *extra)


def _small_body(w_ref, x_ref, o_ref):
    o_ref[...] = _nt_dot(w_ref[...], x_ref[...])


def _proj_small_t(x2d, w_t):
    m, k = x2d.shape
    tm = _pick_tile(m, 2048, 128)
    return pl.pallas_call(
        _small_body,
        grid=(m // tm,),
        in_specs=[pl.BlockSpec((SMALL_ROWS, k), lambda i: (0, 0)), pl.BlockSpec((tm, k), lambda i: (i, 0))],
        out_specs=pl.BlockSpec((SMALL_ROWS, tm), lambda i: (0, i)),
        out_shape=jax.ShapeDtypeStruct((SMALL_ROWS, m), F32),
        compiler_params=_cparams(("parallel",)),
        name="proj_small",
    )(w_t, x2d)


def _prep_body(s_ref, alog_ref, dtb_ref, fb_ref, beta_ref, gc_ref, cumf_ref, carry_ref, *, pad):
    j = pl.program_id(1)

    @pl.when(j == 0)
    def _():
        carry_ref[...] = jnp.zeros_like(carry_ref)

    tl = s_ref.shape[1]
    sm = s_ref[...]
    b, a, f = sm[0:8], sm[8:16], sm[16:24]
    pos = j * tl + lax.broadcasted_iota(I32, (8, tl), 1)
    valid = pos >= pad
    bc = lambda p_ref: p_ref[:, 0:1]
    beta = jnp.where(valid, _sigmoid(b), 0.0)
    g = jnp.where(valid, -jnp.exp(bc(alog_ref)) * _softplus(a + bc(dtb_ref)), 0.0)
    logf = jnp.where(valid, -_softplus(-(f + bc(fb_ref))), 0.0)
    r = lax.broadcasted_iota(I32, (tl, tl), 0)
    c = lax.broadcasted_iota(I32, (tl, tl), 1)
    upper = r <= c
    shift = CHUNK.bit_length() - 1
    same_chunk = lax.shift_right_logical(r, shift) == lax.shift_right_logical(c, shift)
    m_seq = jnp.where(upper, 1.0, 0.0).astype(BF16)
    m_chunk = jnp.where(upper & same_chunk, 1.0, 0.0).astype(BF16)
    beta_ref[0] = beta
    gc_ref[0] = _dot_01(g, m_chunk)
    cum = _dot_01(logf, m_seq) + carry_ref[:, 0:1]
    cumf_ref[0] = cum * LOG2E
    carry_ref[...] = jnp.broadcast_to(cum[:, tl - 1:tl], (8, LANES))


def _prep(small_t, alog, dtb, fb, batch, lp, pad):
    tl = _pick_tile(lp, 512, LANES)
    nb = lp // tl
    out = jax.ShapeDtypeStruct((batch, 8, lp), F32)
    pspec = pl.BlockSpec((8, LANES), lambda b, j: (0, 0))
    ospec = pl.BlockSpec((1, 8, tl), lambda b, j: (b, 0, j))
    return pl.pallas_call(
        functools.partial(_prep_body, pad=pad),
        grid=(batch, nb),
        in_specs=[pl.BlockSpec((SMALL_ROWS, tl), lambda b, j: (0, b * nb + j)), pspec, pspec, pspec],
        out_specs=[ospec, ospec, ospec],
        out_shape=[out, out, out],
        scratch_shapes=[pltpu.VMEM((8, LANES), F32)],
        compiler_params=_cparams(("parallel", "arbitrary")),
        name="gate_prep",
    )(small_t, alog, dtb, fb)


SUB = 16
HALO_ROWS = 16


def _bmm(a, b):
    return jnp.einsum('cij,cjk->cik', a.astype(BF16), b.astype(BF16), preferred_element_type=F32)


def _bmm_nt(a, b):
    return jnp.einsum('cik,cjk->cij', a.astype(BF16), b.astype(BF16), preferred_element_type=F32)


def _wy_body(x_ref, halo_ref, cw_ref, beta_ref, gc_ref, u_ref, w_ref, qd_ref, kt_ref, at_ref, *, heads, cb):
    rows = cb * CHUNK
    aqk = heads * LANES
    halo = halo_ref[0].astype(F32)[HALO_ROWS - 8:]
    xw = jnp.concatenate([halo, x_ref[0].astype(F32)], axis=0)
    cw = cw_ref[...]
    y = cw[CONV_K - 1:CONV_K] * xw[8:]
    for s in range(1, CONV_K):
        y = y + cw[CONV_K - 1 - s:CONV_K - s] * pltpu.roll(xw, s, 0)[8:]
    y = y * _sigmoid(y)
    ii = lax.broadcasted_iota(I32, (CHUNK, CHUNK), 0)
    jj = lax.broadcasted_iota(I32, (CHUNK, CHUNK), 1)
    eye = ii == jj
    incl = ii >= jj
    sub_shift = SUB.bit_length() - 1
    same_sub = lax.shift_right_logical(ii, sub_shift) == lax.shift_right_logical(jj, sub_shift)
    eye_f = jnp.where(eye, 1.0, 0.0)

    def heads_batched(off):
        return jnp.concatenate([y[:, off + h * LANES:off + (h + 1) * LANES].reshape(cb, CHUNK, LANES)
                                for h in range(heads)], axis=0)

    q, k, v = heads_batched(0), heads_batched(aqk), heads_batched(2 * aqk)
    q = q * lax.rsqrt(jnp.sum(q * q, axis=-1, keepdims=True) + EPS) * (LANES ** -0.5)
    k = k * lax.rsqrt(jnp.sum(k * k, axis=-1, keepdims=True) + EPS)
    b_row = jnp.concatenate([beta_ref[0, h] for h in range(heads)], axis=0)
    g_row = jnp.concatenate([gc_ref[0, h] for h in range(heads)], axis=0)
    b_col = jnp.sum(jnp.where(eye, b_row, 0.0), axis=2, keepdims=True)
    g_col = jnp.sum(jnp.where(eye, g_row, 0.0), axis=2, keepdims=True)
    gamma = jnp.where(incl, jnp.exp(jnp.where(incl, g_col - g_row, 0.0)), 0.0)
    kb = k * b_col
    a_mat = jnp.where(ii > jj, _bmm_nt(kb, k) * gamma, 0.0)
    attn = _bmm_nt(q, k) * gamma
    x = jnp.where(same_sub, -a_mat, 0.0)
    t_d = eye_f + x
    for _ in range(3):
        x = _bmm(x, x)
        t_d = t_d + _bmm(t_d, x)
    m1 = _bmm(t_d, jnp.where(same_sub, 0.0, a_mat))
    m2 = _bmm(m1, m1)
    t = _bmm(eye_f - m1 + m2 - _bmm(m1, m2), t_d)
    e_g = jnp.exp(g_col)
    sol = _bmm(t, jnp.concatenate([v * b_col, kb * e_g], axis=2))
    g_last = g_row[:, :, CHUNK - 1:CHUNK]
    qd = q * e_g
    kt = k * jnp.exp(g_last - g_col)
    at = jnp.concatenate([attn, jnp.zeros_like(attn)], axis=2)
    for h in range(heads):
        hs = slice(h * LANES, (h + 1) * LANES)
        bs = slice(h * cb, (h + 1) * cb)
        u_ref[0, :, hs] = sol[bs, :, :LANES].reshape(rows, LANES)
        w_ref[0, :, hs] = sol[bs, :, LANES:].reshape(rows, LANES).astype(BF16)
        qd_ref[0, :, hs] = qd[bs].reshape(rows, LANES).astype(BF16)
        kt_ref[0, :, hs] = kt[bs].reshape(rows, LANES).astype(BF16)
        at_ref[0, :, hs] = at[bs].reshape(rows, LANES).astype(BF16)


def _wy(qkv_a, conv_w, beta5, gc5, batch, lp, heads):
    n_chunks = lp // CHUNK
    cb = max(c for c in range(1, 7) if n_chunks % c == 0)
    rows = cb * CHUNK
    width = heads * LANES
    ospec = pl.BlockSpec((1, rows, width), lambda b, i: (b, i, 0))
    rspec = pl.BlockSpec((1, 8, cb, 1, CHUNK), lambda b, i: (b, 0, i, 0, 0))
    f32o = jax.ShapeDtypeStruct((batch, lp, width), F32)
    b16o = jax.ShapeDtypeStruct((batch, lp, width), BF16)
    return pl.pallas_call(
        functools.partial(_wy_body, heads=heads, cb=cb),
        grid=(batch, n_chunks // cb),
        in_specs=[pl.BlockSpec((1, rows, 3 * width), lambda b, i: (b, i, 0)),
                  pl.BlockSpec((1, HALO_ROWS, 3 * width),
                               lambda b, i: (b, jnp.maximum(i * (rows // HALO_ROWS) - 1, 0), 0)),
                  pl.BlockSpec((CONV_K, 3 * width), lambda b, i: (0, 0)), rspec, rspec],
        out_specs=[ospec] * 5,
        out_shape=[f32o, b16o, b16o, b16o, b16o],
        compiler_params=_cparams(("parallel", "parallel")),
        name="delta_rule_wy",
    )(qkv_a, qkv_a, conv_w, beta5, gc5)


def _scan_body(u_ref, w_ref, qd_ref, kt_ref, at_ref, gc_ref, zs_ref, onw_ref, o_ref, s_s, *, heads):
    @pl.when(pl.program_id(1) == 0)
    def _():
        s_s[...] = jnp.zeros_like(s_s)

    hsl = [slice(h * LANES, (h + 1) * LANES) for h in range(heads)]
    for c in range(BLOCK // CHUNK):
        rs = slice(c * CHUNK, (c + 1) * CHUNK)
        s = [s_s[h] for h in range(heads)]
        ws = [jnp.dot(jnp.concatenate([w_ref[0, rs, hs], qd_ref[0, rs, hs]], axis=0), s[h].astype(BF16),
                      preferred_element_type=F32) for h, hs in enumerate(hsl)]
        vb = [(u_ref[0, rs, hs] - ws[h][:CHUNK]).astype(BF16) for h, hs in enumerate(hsl)]
        upd = [_tn_dot(kt_ref[0, rs, hs], vb[h]) for h, hs in enumerate(hsl)]
        o = [ws[h][CHUNK:] + jnp.dot(at_ref[0, rs, h * LANES:h * LANES + CHUNK], vb[h], preferred_element_type=F32)
             for h in range(heads)]
        for h, hs in enumerate(hsl):
            dec = jnp.exp(gc_ref[0, h, c][:, CHUNK - 1:CHUNK])
            s_s[h] = s[h] * dec + upd[h]
            on = o[h] * lax.rsqrt(jnp.mean(o[h] * o[h], axis=-1, keepdims=True) + EPS) * onw_ref[...]
            o_ref[0, rs, hs] = (on * zs_ref[0, rs, hs].astype(F32)).astype(o_ref.dtype)


def _scan(u, w, qd, kt, at, gc5, zs, onw, batch, lp, heads):
    seq = lp - BLOCK
    width = heads * LANES
    ispec = pl.BlockSpec((1, BLOCK, width), lambda b, i: (b, i, 0))
    xspec = pl.BlockSpec((1, BLOCK, width), lambda b, i: (b, jnp.maximum(i - 1, 0), 0))
    return pl.pallas_call(
        functools.partial(_scan_body, heads=heads),
        grid=(batch, lp // BLOCK),
        in_specs=[ispec] * 5 + [pl.BlockSpec((1, 8, BLOCK // CHUNK, 1, CHUNK), lambda b, i: (b, 0, i, 0, 0)),
                                xspec, pl.BlockSpec((1, LANES), lambda b, i: (0, 0))],
        out_specs=xspec,
        out_shape=jax.ShapeDtypeStruct((batch, seq, width), BF16),
        scratch_shapes=[pltpu.VMEM((heads, LANES, LANES), F32)],
        compiler_params=_cparams(("parallel", "arbitrary")),
        name="delta_rule_scan",
    )(u, w, qd, kt, at, gc5, zs, onw)


def _fox_body(q_ref, k_ref, v_ref, cum_ref, o_ref, *, tq, pad):
    i = pl.program_id(2)
    q0 = pl.multiple_of(BLOCK + i * tq, BLOCK)
    th = tq // 2
    q = q_ref[0, pl.ds(q0, tq), :]

    def tile(qx, k0, size, mask, carry):
        m, l, acc = carry
        s = _nt_dot(qx, k_ref[0, pl.ds(k0, size), :]) - cum_ref[0, :, pl.ds(k0, size)]
        if mask is not None:
            s = jnp.where(mask, s, NEG_INF)
        m_new = jnp.maximum(m, jnp.max(s, axis=1, keepdims=True))
        alpha = jnp.exp2(m - m_new)
        p = jnp.exp2(s - m_new)
        l = alpha * l + jnp.sum(p, axis=1, keepdims=True)
        acc = alpha * acc + jnp.dot(p.astype(BF16), v_ref[0, pl.ds(k0, size), :], preferred_element_type=F32)
        return m_new, l, acc

    carry = (jnp.full((tq, 1), NEG_INF, F32), jnp.zeros((tq, 1), F32), jnp.zeros((tq, LANES), F32))
    carry = tile(q, 0, BLOCK, lax.broadcasted_iota(I32, (tq, BLOCK), 1) >= pad, carry)
    carry = lax.fori_loop(0, i, lambda j, c: tile(q, pl.multiple_of(BLOCK + j * tq, BLOCK), tq, None, c), carry)
    m, l, acc = tile(q, q0, th, lax.broadcasted_iota(I32, (tq, th), 1) <= lax.broadcasted_iota(I32, (tq, th), 0), carry)
    o_ref[0, :th, :] = (acc[:th] / l[:th]).astype(o_ref.dtype)
    causal = lax.broadcasted_iota(I32, (th, th), 1) <= lax.broadcasted_iota(I32, (th, th), 0)
    m, l, acc = tile(q[th:], q0 + th, th, causal, (m[th:], l[th:], acc[th:]))
    o_ref[0, th:, :] = (acc / l).astype(o_ref.dtype)


def _fox(qkv, cumf3, batch, lp, heads, pad):
    seq = lp - BLOCK
    tq = _pick_tile(seq, 1024, 2 * BLOCK)
    kvspec = lambda off: pl.BlockSpec((1, lp, LANES), lambda b, h, i: (b, 0, off + h))
    return pl.pallas_call(
        functools.partial(_fox_body, tq=tq, pad=pad),
        grid=(batch, heads, seq // tq),
        in_specs=[kvspec(0), kvspec(heads), kvspec(2 * heads),
                  pl.BlockSpec((1, 1, lp), lambda b, h, i: (b * 8 + h, 0, 0))],
        out_specs=pl.BlockSpec((1, tq, LANES), lambda b, h, i: (b, i, h)),
        out_shape=jax.ShapeDtypeStruct((batch, seq, heads * LANES), BF16),
        compiler_params=_cparams(("parallel", "parallel", "arbitrary")),
        name="forgetting_attention",
    )(qkv, qkv, qkv, cumf3)


def _merge_body(oa_ref, ob_ref, g_ref, x_ref, wa_ref, wb_ref, wo_ref, nw_ref, res_ref, xn_ref):
    d = x_ref.shape[1]
    th = x_ref.shape[0] // 2
    halves = [slice(0, th), slice(th, 2 * th)]
    ya = [jnp.dot(oa_ref[r, :], wa_ref[...], preferred_element_type=F32) for r in halves]
    yb = [jnp.dot(ob_ref[r, :], wb_ref[...], preferred_element_type=F32) for r in halves]
    mix = []
    for c, r in enumerate(halves):
        g = g_ref[r, :].astype(F32)
        mix.append((g[:, :d] * ya[c] + g[:, d:] * yb[c]).astype(BF16))
    y = [jnp.dot(mix[c], wo_ref[...], preferred_element_type=F32) for c in range(2)]
    for c, r in enumerate(halves):
        res = x_ref[r, :] + y[c]
        res_ref[r, :] = res
        xn = res * lax.rsqrt(jnp.mean(res * res, axis=-1, keepdims=True) + EPS) * nw_ref[...]
        xn_ref[r, :] = xn.astype(xn_ref.dtype)


def _merge(oa, ob, gates, x2d, wa, wb, wo, nw):
    m, d = x2d.shape
    bw = oa.shape[1]
    tm = _pick_tile(m, 512, 128)
    row = lambda c: pl.BlockSpec((tm, c), lambda i: (i, 0))
    full = lambda r, c: pl.BlockSpec((r, c), lambda i: (0, 0))
    return pl.pallas_call(
        _merge_body,
        grid=(m // tm,),
        in_specs=[row(bw), row(bw), row(2 * d), row(d), full(bw, d), full(bw, d), full(d, d), full(1, d)],
        out_specs=[row(d), row(d)],
        out_shape=[jax.ShapeDtypeStruct((m, d), F32), jax.ShapeDtypeStruct((m, d), BF16)],
        compiler_params=_cparams(("parallel",)),
        name="merge_out_proj",
    )(oa, ob, gates, x2d, wa, wb, wo, nw)


_CAND_GROUPS = ((0, 0, 8, 8), (0, 8, 16, 8), (1, 0, 8, 8), (2, 0, 8, 5), (3, 0, 8, 4), (4, 0, 8, 3),
                (5, 0, 8, 2), (6, 0, 8, 2), (7, 0, 8, 2))


def _sort_network(n):
    pairs, p = [], 1
    while p < n:
        k = p
        while k >= 1:
            for j in range(k % p, n - k, 2 * k):
                for i in range(min(k, n - j - k)):
                    if (i + j) // (2 * p) == (i + j + k) // (2 * p):
                        pairs.append((i + j, i + j + k))
            k //= 2
        p *= 2
    return pairs


def _better(a, b):
    return (a[0] > b[0]) | ((a[0] == b[0]) & (a[1] < b[1]))


def _top_rows(groups, n, out_refs):
    depth = len(groups)
    size = 1
    while size < depth:
        size *= 2
    g = list(groups) + [None] * (size - depth)
    for i, j in _sort_network(size):
        if g[j] is None:
            continue
        if g[i] is None:
            g[i], g[j] = g[j], None
            continue
        take = _better(g[j], g[i])
        g[i], g[j] = (tuple(jnp.where(take, y, x) for x, y in zip(g[i], g[j])),
                      tuple(jnp.where(take, x, y) for x, y in zip(g[i], g[j])))
    g = g[:depth]
    worst = (-jnp.inf, 1e9)
    for kk in range(n):
        best = g[0]
        for shift in (4, 2, 1):
            other = tuple(pltpu.roll(x, shift, 0) for x in best)
            take = _better(other, best)
            best = tuple(jnp.where(take, y, x) for x, y in zip(best, other))
        for ref, x in zip(out_refs, best):
            ref[kk:kk + 1, :] = x[0:1]
        win = g[0][1] == best[1]
        for d in range(min(depth, n - kk - 1)):
            nxt = g[d + 1] if d + 1 < depth else tuple(jnp.full_like(x, w) for x, w in zip(g[d], worst + (0.0,) * 8))
            g[d] = tuple(jnp.where(win, y, x) for x, y in zip(g[d], nxt))


def _route_body(xn_ref, wqt_ref, keys_ref, g_ref, a_ref, b_ref, qt_s, gt_s, at_s, bt_s, v_s, i_s, sc_s, fl_s, ea_s,
                eb_s, *, heads, n_keys):
    tm = xn_ref.shape[0]
    qt_s[...] = _nt_dot(wqt_ref[...], xn_ref[...]).astype(BF16)

    def row_ids(shape):
        return lax.broadcasted_iota(I32, shape, 0) + lax.shift_right_logical(lax.broadcasted_iota(I32, shape, 1), 31)

    key_iota = row_ids((n_keys, LANES)).astype(F32)
    sub = row_ids((8, LANES))

    def head(h, carry):
        st = []
        for p in range(2):
            qhp = qt_s[pl.ds(pl.multiple_of(h * 2 * LANES + p * LANES, LANES), LANES), :]
            kk = keys_ref[pl.ds(pl.multiple_of((p * heads + h) * n_keys, n_keys), n_keys), :]
            st.append(jnp.dot(kk, qhp, preferred_element_type=F32))
        for c in range(tm // LANES):
            cs = slice(c * LANES, (c + 1) * LANES)
            for p in range(2):
                ps = slice(p * tm + c * LANES, p * tm + (c + 1) * LANES)
                _top_rows([(st[p][8 * d:8 * d + 8, cs], key_iota[8 * d:8 * d + 8]) for d in range(n_keys // 8)], TOPK,
                          (v_s.at[:, ps], i_s.at[:, ps]))
        for c in range(tm // LANES):
            cs = slice(c * LANES, (c + 1) * LANES)
            v1, v2 = v_s[:, cs], v_s[:, tm + c * LANES:tm + (c + 1) * LANES]
            i1, i2 = i_s[:, cs], i_s[:, tm + c * LANES:tm + (c + 1) * LANES]
            groups = []
            for (a, b0, b1, nvalid) in _CAND_GROUPS:
                groups.append((jnp.where(sub < nvalid, v1[a:a + 1] + v2[b0:b1], -jnp.inf),
                               (a * TOPK + b0 + sub).astype(F32), jnp.broadcast_to(i1[a:a + 1], (8, LANES)), i2[b0:b1]))
            groups.append((v1[8:16] + v2[0:1], ((8 + sub) * TOPK).astype(F32), i1[8:16],
                           jnp.broadcast_to(i2[0:1], (8, LANES))))
            _top_rows(groups, TOPK, (sc_s.at[:, cs], fl_s.at[:, cs], ea_s.at[:, cs], eb_s.at[:, cs]))
        sc = sc_s[...]
        e = jnp.exp(sc - sc[0:1])
        row = pl.ds(pl.multiple_of(h * TOPK, TOPK), TOPK)
        gt_s[row, :] = e / jnp.sum(e, axis=0, keepdims=True)
        at_s[row, :] = ea_s[...]
        bt_s[row, :] = eb_s[...]
        return carry

    lax.fori_loop(0, heads, head, 0)
    g_ref[...] = gt_s[...].T
    a_ref[...] = at_s[...].T
    b_ref[...] = bt_s[...].T


def _route(xn, wq_t, keys2d, heads, n_keys):
    t, d = xn.shape
    tm = _pick_tile(t, 512, LANES)
    hk = heads * TOPK
    out = jax.ShapeDtypeStruct((t, hk), F32)
    ospec = pl.BlockSpec((tm, hk), lambda i: (i, 0))
    tmp = lambda w: pltpu.VMEM((TOPK, w), F32)
    return pl.pallas_call(
        functools.partial(_route_body, heads=heads, n_keys=n_keys),
        grid=(t // tm,),
        in_specs=[pl.BlockSpec((tm, d), lambda i: (i, 0)),
                  pl.BlockSpec(wq_t.shape, lambda i: (0, 0)),
                  pl.BlockSpec(keys2d.shape, lambda i: (0, 0))],
        out_specs=[ospec, ospec, ospec],
        out_shape=[out, out, out],
        scratch_shapes=[pltpu.VMEM((wq_t.shape[0], tm), BF16)] + [pltpu.VMEM((hk, tm), F32) for _ in range(3)]
        + [tmp(2 * tm), tmp(2 * tm), tmp(tm), tmp(tm), tmp(tm), tmp(tm)],
        compiler_params=_cparams(("parallel",)),
        name="peer_route",
    )(xn, wq_t, keys2d)


def _gelu(x):
    return 0.5 * x * (1.0 + lax.erf(x * (2.0 ** -0.5)))


W_PITCH_PAD = 8
MXU_DIM = 256


def _expert_body(xn_ref, res_ref, g_ref, a_ref, b_ref, ut_ref, v_ref, o_ref, w_s, acc_s, *, n_keys, n_steps):
    j = pl.program_id(1)
    tm = xn_ref.shape[0]
    te = ut_ref.shape[1]
    half_keys = n_keys // 2
    half_steps = n_steps // 2
    pitch = half_keys + W_PITCH_PAD

    @pl.when(j == 0)
    def _():
        acc_s[...] = jnp.zeros_like(acc_s)

    @pl.when(lax.rem(j, half_steps) == 0)
    def _():
        base = jnp.where(j >= half_steps, half_keys, 0)
        key_a = (lax.broadcasted_iota(I32, (half_keys, g_ref.shape[1]), 0) + base).astype(BF16)
        key_b = lax.broadcasted_iota(I32, (n_keys, g_ref.shape[1]), 0).astype(BF16)

        def token(t, carry):
            a_row = a_ref[pl.ds(t, 1), :].astype(BF16)
            b_row = b_ref[pl.ds(t, 1), :].astype(BF16)
            g_row = g_ref[pl.ds(t, 1), :].astype(BF16)
            pt = jnp.where(key_a == a_row, g_row, jnp.zeros_like(g_row))
            qt = jnp.where(key_b == b_row, jnp.ones_like(g_row), jnp.zeros_like(g_row))
            w_s[pl.ds(pl.multiple_of(t * pitch, 8), half_keys), :] = _nt_dot(pt, qt)
            return carry

        lax.fori_loop(0, tm, token, 0, unroll=64)

    n_slab = te // n_keys
    act = jnp.dot(xn_ref[...], ut_ref[...], preferred_element_type=F32)
    coef = []
    for s in range(n_slab):
        w_slab = w_s[pl.ds(lax.rem(j, half_steps) * n_slab + s, tm, stride=pitch), :]
        coef.append((_gelu(act[:, s * n_keys:(s + 1) * n_keys]) * w_slab).astype(BF16))
    acc_s[...] += jnp.dot(jnp.concatenate(coef, axis=1), v_ref[...], preferred_element_type=F32)

    @pl.when(j == pl.num_programs(1) - 1)
    def _():
        o_ref[...] = res_ref[...] + acc_s[...]


def _experts(xn, res, g, a, b, u_t, v, n_keys):
    t, d = xn.shape
    ne = v.shape[0]
    hk = g.shape[1]
    tm = _pick_tile(t, 512, 2 * LANES)
    te = _pick_tile(ne, 2048, MXU_DIM)
    tok = lambda c: pl.BlockSpec((tm, c), lambda i, j: (i, 0))
    n_steps = ne // te
    assert n_steps % 2 == 0 and (n_steps // 2) * (te // n_keys) == n_keys // 2
    return pl.pallas_call(
        functools.partial(_expert_body, n_keys=n_keys, n_steps=n_steps),
        grid=(t // tm, n_steps),
        in_specs=[tok(d), tok(d), tok(hk), tok(hk), tok(hk),
                  pl.BlockSpec((d, te), lambda i, j: (0, j)), pl.BlockSpec((te, d), lambda i, j: (j, 0))],
        out_specs=tok(d),
        out_shape=jax.ShapeDtypeStruct((t, d), F32),
        scratch_shapes=[pltpu.VMEM(((n_keys // 2 + W_PITCH_PAD) * tm, n_keys), F32),
                        pltpu.VMEM((tm, d), F32)],
        compiler_params=_cparams(("parallel", "arbitrary")),
        name="peer_experts",
    )(xn, res, g, a, b, u_t, v)


def _layer(x, meta_tokens, norm_mix, w_in, conv_w, a_log, dt_bias, o_norm_a, q_norm_b, k_norm_b, f_bias,
           w_branch, w_out, norm_ffn, peer_wq, peer_sub_keys, expert_u, expert_v):
    batch, seq, d = x.shape
    n_meta = meta_tokens.shape[0]
    pad = BLOCK - n_meta
    lp = BLOCK + seq
    ha, hb = a_log.shape[0], f_bias.shape[0]
    aqk = ha * LANES
    bw = hb * LANES
    assert ha <= 8 and hb <= 8 and seq % BLOCK == 0 and conv_w.shape == (CONV_K, 3 * aqk)
    assert w_in.shape[1] == 4 * aqk + 2 * ha + 3 * bw + hb + 2 * d
    c_z, c_b, c_a = 3 * aqk, 4 * aqk, 4 * aqk + ha
    c_qkvb = 4 * aqk + 2 * ha
    c_f = c_qkvb + 3 * bw
    c_g = c_f + hb

    x2d = x.reshape(batch * seq, d)
    hn_x = _rmsnorm(x2d, norm_mix, BF16)
    hn_m = _rmsnorm(meta_tokens, norm_mix, BF16)
    hn_p = jnp.concatenate([jnp.zeros((batch, pad, d), BF16), jnp.broadcast_to(hn_m[None], (batch, n_meta, d)),
                            hn_x.reshape(batch, seq, d)], axis=1).reshape(batch * lp, d)

    wb16 = w_in.astype(BF16)
    qkv_a = _proj(_proj_plain_body, hn_p, wb16[:, :c_z], BF16, _pick_tile(c_z, 1024, LANES), name="proj_qkv_a")
    zs = _proj(_proj_silu_body, hn_x, wb16[:, c_z:c_b], BF16, _pick_tile(aqk, 1024, LANES), name="proj_z")
    scale = LANES ** -0.5 * LOG2E
    nw = jnp.concatenate([jnp.tile(q_norm_b * scale, hb), jnp.tile(k_norm_b, hb), jnp.ones((bw,), F32)]).reshape(1, 3 * bw)
    qkv_b = _proj(_proj_qkvb_body, hn_p, wb16[:, c_qkvb:c_f], BF16, bw, extra=(nw,), name="proj_qkv_b")
    gates = _proj(_proj_sigmoid_body, hn_x, wb16[:, c_g:], BF16, _pick_tile(2 * d, 1024, LANES), name="proj_gates")

    def rows8(w):
        return jnp.pad(w.T, ((0, 8 - w.shape[1]), (0, 0)))

    w_small = jnp.concatenate([rows8(wb16[:, c_b:c_a]), rows8(wb16[:, c_a:c_qkvb]), rows8(wb16[:, c_f:c_g]),
                               jnp.zeros((SMALL_ROWS - 24, d), BF16)], axis=0)
    small_t = _proj_small_t(hn_p, w_small)

    def lanes8(p):
        return jnp.broadcast_to(jnp.pad(p.astype(F32), (0, 8 - p.shape[0]))[:, None], (8, LANES))

    beta, gc, cumf = _prep(small_t, lanes8(a_log), lanes8(dt_bias), lanes8(f_bias), batch, lp, pad)
    n_chunks = lp // CHUNK
    beta5 = beta.reshape(batch, 8, n_chunks, 1, CHUNK)
    gc5 = gc.reshape(batch, 8, n_chunks, 1, CHUNK)
    u, w, qd, kt, at = _wy(qkv_a.reshape(batch, lp, 3 * aqk), conv_w, beta5, gc5, batch, lp, ha)
    o_a = _scan(u, w, qd, kt, at, gc5, zs.reshape(batch, seq, aqk), o_norm_a.reshape(1, LANES).astype(F32),
                batch, lp, ha)

    o_b = _fox(qkv_b.reshape(batch, lp, 3 * bw), cumf.reshape(batch * 8, 1, lp), batch, lp, hb, pad)

    res, xn = _merge(o_a.reshape(batch * seq, aqk), o_b.reshape(batch * seq, bw), gates, x2d,
                     w_branch[0].astype(BF16), w_branch[1].astype(BF16), w_out.astype(BF16),
                     norm_ffn.reshape(1, d).astype(F32))

    hp, n_keys = peer_sub_keys.shape[1], peer_sub_keys.shape[2]
    assert peer_sub_keys.shape[3] == LANES and n_keys == LANES and peer_wq.shape[1] == hp * 2 * LANES
    g_w, a_k, b_k = _route(xn, peer_wq.T.astype(BF16), peer_sub_keys.reshape(2 * hp * n_keys, LANES).astype(BF16),
                           hp, n_keys)
    out = _experts(xn, res, g_w, a_k, b_k, expert_u.T.astype(BF16), expert_v.astype(BF16), n_keys)
    return out.reshape(batch, seq, d)


def kernel(x, meta_tokens, norm_mix, w_in, conv_w, a_log, dt_bias, o_norm_a, q_norm_b, k_norm_b, f_bias,
           w_branch, w_out, norm_ffn, peer_wq, peer_sub_keys, expert_u, expert_v):
    assert norm_mix.shape[0] == 1, "single-layer block"
    return _layer(x, meta_tokens, norm_mix[0], w_in[0], conv_w[0], a_log[0], dt_bias[0], o_norm_a[0],
                  q_norm_b[0], k_norm_b[0], f_bias[0], w_branch[0], w_out[0], norm_ffn[0], peer_wq[0],
                  peer_sub_keys[0], expert_u[0], expert_v[0])
```

```python
import functools
import math

import jax
import jax.numpy as jnp
from jax import lax
from jax.experimental import pallas as pl
from jax.experimental.pallas import tpu as pltpu

F32, BF16, I32 = jnp.float32, jnp.bfloat16, jnp.int32
EPS = 1e-6
NEG_INF = -1e30
LOG2E = math.log2(math.e)
LANES = 128
BLOCK = 128
CHUNK = 64
CONV_K = 4
TOPK = 16
SMALL_ROWS = 32
VMEM_LIMIT = 56 * 1024 * 1024


def _cparams(sem):
    return pltpu.CompilerParams(dimension_semantics=sem, vmem_limit_bytes=VMEM_LIMIT)


def _pick_tile(n, target, mult):
    best = None
    for t in range(mult, min(n, target) + 1, mult):
        if n % t == 0:
            best = t
    assert best is not None, (n, target, mult)
    return best


def _nt_dot(a, b):
    return lax.dot_general(a, b, (((1,), (1,)), ((), ())), preferred_element_type=F32)


def _tn_dot(a, b):
    return lax.dot_general(a, b, (((0,), (0,)), ((), ())), preferred_element_type=F32)


def _split3(x):
    hi = x.astype(BF16)
    r = x - hi.astype(F32)
    mid = r.astype(BF16)
    lo = (r - mid.astype(F32)).astype(BF16)
    return hi, mid, lo


def _dot_01(x, m01):
    hi, mid, lo = _split3(x)
    d = functools.partial(jnp.dot, preferred_element_type=F32)
    return d(hi, m01) + (d(mid, m01) + d(lo, m01))


def _sigmoid(x):
    return 1.0 / (1.0 + jnp.exp(-x))


def _softplus(x):
    return jnp.maximum(x, 0.0) + jnp.log1p(jnp.exp(-jnp.abs(x)))


def _rmsnorm_body(x_ref, w_ref, o_ref):
    x = x_ref[...].astype(F32)
    y = x * lax.rsqrt(jnp.mean(x * x, axis=-1, keepdims=True) + EPS)
    o_ref[...] = (y * w_ref[...]).astype(o_ref.dtype)


def _rmsnorm(x2d, w, out_dtype):
    m, d = x2d.shape
    tm = _pick_tile(m, 1024, 8)
    return pl.pallas_call(
        _rmsnorm_body,
        grid=(m // tm,),
        in_specs=[pl.BlockSpec((tm, d), lambda i: (i, 0)), pl.BlockSpec((1, d), lambda i: (0, 0))],
        out_specs=pl.BlockSpec((tm, d), lambda i: (i, 0)),
        out_shape=jax.ShapeDtypeStruct((m, d), out_dtype),
        compiler_params=_cparams(("parallel",)),
        name="rmsnorm",
    )(x2d, w.reshape(1, d).astype(F32))


def _proj_plain_body(x_ref, w_ref, o_ref):
    o_ref[...] = jnp.dot(x_ref[...], w_ref[...], preferred_element_type=F32).astype(o_ref.dtype)


def _row_halves(x_ref, w_ref):
    th = x_ref.shape[0] // 2
    rows = [slice(0, th), slice(th, 2 * th)]
    return [(r, jnp.dot(x_ref[r, :], w_ref[...], preferred_element_type=F32)) for r in rows]


def _proj_silu_body(x_ref, w_ref, o_ref):
    for r, y in _row_halves(x_ref, w_ref):
        o_ref[r, :] = (y * _sigmoid(y)).astype(o_ref.dtype)


def _proj_sigmoid_body(x_ref, w_ref, o_ref):
    for r, y in _row_halves(x_ref, w_ref):
        o_ref[r, :] = _sigmoid(y).astype(o_ref.dtype)


def _proj_qkvb_body(x_ref, w_ref, nw_ref, o_ref):
    do_norm = pl.program_id(1) < 2
    for r, y in _row_halves(x_ref, w_ref):
        for h in range(y.shape[1] // LANES):
            sl = slice(h * LANES, (h + 1) * LANES)
            seg = y[:, sl]
            rs = lax.rsqrt(jnp.mean(seg * seg, axis=-1, keepdims=True) + EPS)
            rs = jnp.where(do_norm, rs, 1.0)
            o_ref[r, sl] = (seg * rs * nw_ref[:, sl]).astype(o_ref.dtype)


def _proj(body, x2d, w, out_dtype, tn, extra=(), name="proj"):
    m, k = x2d.shape
    n = w.shape[1]
    tm = _pick_tile(m, 1536, 128)
    assert n % tn == 0
    in_specs = [pl.BlockSpec((tm, k), lambda i, j: (i, 0)), pl.BlockSpec((k, tn), lambda i, j: (0, j))]
    in_specs += [pl.BlockSpec((1, tn), lambda i, j: (0, j)) for _ in extra]
    return pl.pallas_call(
        body,
        grid=(m // tm, n // tn),
        in_specs=in_specs,
        out_specs=pl.BlockSpec((tm, tn), lambda i, j: (i, j)),
        out_shape=jax.ShapeDtypeStruct((m, n), out_dtype),
        compiler_params=_cparams(("parallel", "arbitrary")),
        name=name,
    )(x2d, w, *extra)


def _small_body(w_ref, x_ref, o_ref):
    o_ref[...] = _nt_dot(w_ref[...], x_ref[...])


def _proj_small_t(x2d, w_t):
    m, k = x2d.shape
    tm = _pick_tile(m, 2048, 128)
    return pl.pallas_call(
        _small_body,
        grid=(m // tm,),
        in_specs=[pl.BlockSpec((SMALL_ROWS, k), lambda i: (0, 0)), pl.BlockSpec((tm, k), lambda i: (i, 0))],
        out_specs=pl.BlockSpec((SMALL_ROWS, tm), lambda i: (0, i)),
        out_shape=jax.ShapeDtypeStruct((SMALL_ROWS, m), F32),
        compiler_params=_cparams(("parallel",)),
        name="proj_small",
    )(w_t, x2d)


def _prep_body(s_ref, alog_ref, dtb_ref, fb_ref, beta_ref, gc_ref, cumf_ref, carry_ref, *, pad):
    j = pl.program_id(1)

    @pl.when(j == 0)
    def _():
        carry_ref[...] = jnp.zeros_like(carry_ref)

    tl = s_ref.shape[1]
    sm = s_ref[...]
    b, a, f = sm[0:8], sm[8:16], sm[16:24]
    pos = j * tl + lax.broadcasted_iota(I32, (8, tl), 1)
    valid = pos >= pad
    bc = lambda p_ref: p_ref[:, 0:1]
    beta = jnp.where(valid, _sigmoid(b), 0.0)
    g = jnp.where(valid, -jnp.exp(bc(alog_ref)) * _softplus(a + bc(dtb_ref)), 0.0)
    logf = jnp.where(valid, -_softplus(-(f + bc(fb_ref))), 0.0)
    r = lax.broadcasted_iota(I32, (tl, tl), 0)
    c = lax.broadcasted_iota(I32, (tl, tl), 1)
    upper = r <= c
    shift = CHUNK.bit_length() - 1
    same_chunk = lax.shift_right_logical(r, shift) == lax.shift_right_logical(c, shift)
    m_seq = jnp.where(upper, 1.0, 0.0).astype(BF16)
    m_chunk = jnp.where(upper & same_chunk, 1.0, 0.0).astype(BF16)
    beta_ref[0] = beta
    gc_ref[0] = _dot_01(g, m_chunk)
    cum = _dot_01(logf, m_seq) + carry_ref[:, 0:1]
    cumf_ref[0] = cum * LOG2E
    carry_ref[...] = jnp.broadcast_to(cum[:, tl - 1:tl], (8, LANES))


def _prep(small_t, alog, dtb, fb, batch, lp, pad):
    tl = _pick_tile(lp, 512, LANES)
    nb = lp // tl
    out = jax.ShapeDtypeStruct((batch, 8, lp), F32)
    pspec = pl.BlockSpec((8, LANES), lambda b, j: (0, 0))
    ospec = pl.BlockSpec((1, 8, tl), lambda b, j: (b, 0, j))
    return pl.pallas_call(
        functools.partial(_prep_body, pad=pad),
        grid=(batch, nb),
        in_specs=[pl.BlockSpec((SMALL_ROWS, tl), lambda b, j: (0, b * nb + j)), pspec, pspec, pspec],
        out_specs=[ospec, ospec, ospec],
        out_shape=[out, out, out],
        scratch_shapes=[pltpu.VMEM((8, LANES), F32)],
        compiler_params=_cparams(("parallel", "arbitrary")),
        name="gate_prep",
    )(small_t, alog, dtb, fb)


SUB = 16
HALO_ROWS = 16


def _bmm(a, b):
    return jnp.einsum('cij,cjk->cik', a.astype(BF16), b.astype(BF16), preferred_element_type=F32)


def _bmm_nt(a, b):
    return jnp.einsum('cik,cjk->cij', a.astype(BF16), b.astype(BF16), preferred_element_type=F32)


def _wy_body(x_ref, halo_ref, cw_ref, beta_ref, gc_ref, u_ref, w_ref, qd_ref, kt_ref, at_ref, *, heads, cb):
    rows = cb * CHUNK
    aqk = heads * LANES
    halo = halo_ref[0].astype(F32)[HALO_ROWS - 8:]
    xw = jnp.concatenate([halo, x_ref[0].astype(F32)], axis=0)
    cw = cw_ref[...]
    y = cw[CONV_K - 1:CONV_K] * xw[8:]
    for s in range(1, CONV_K):
        y = y + cw[CONV_K - 1 - s:CONV_K - s] * pltpu.roll(xw, s, 0)[8:]
    y = y * _sigmoid(y)
    ii = lax.broadcasted_iota(I32, (CHUNK, CHUNK), 0)
    jj = lax.broadcasted_iota(I32, (CHUNK, CHUNK), 1)
    eye = ii == jj
    incl = ii >= jj
    sub_shift = SUB.bit_length() - 1
    same_sub = lax.shift_right_logical(ii, sub_shift) == lax.shift_right_logical(jj, sub_shift)
    eye_f = jnp.where(eye, 1.0, 0.0)

    def heads_batched(off):
        return jnp.concatenate([y[:, off + h * LANES:off + (h + 1) * LANES].reshape(cb, CHUNK, LANES)
                                for h in range(heads)], axis=0)

    q, k, v = heads_batched(0), heads_batched(aqk), heads_batched(2 * aqk)
    q = q * lax.rsqrt(jnp.sum(q * q, axis=-1, keepdims=True) + EPS) * (LANES ** -0.5)
    k = k * lax.rsqrt(jnp.sum(k * k, axis=-1, keepdims=True) + EPS)
    b_row = jnp.concatenate([beta_ref[0, h] for h in range(heads)], axis=0)
    g_row = jnp.concatenate([gc_ref[0, h] for h in range(heads)], axis=0)
    b_col = jnp.sum(jnp.where(eye, b_row, 0.0), axis=2, keepdims=True)
    g_col = jnp.sum(jnp.where(eye, g_row, 0.0), axis=2, keepdims=True)
    gamma = jnp.where(incl, jnp.exp(jnp.where(incl, g_col - g_row, 0.0)), 0.0)
    kb = k * b_col
    a_mat = jnp.where(ii > jj, _bmm_nt(kb, k) * gamma, 0.0)
    attn = _bmm_nt(q, k) * gamma
    x = jnp.where(same_sub, -a_mat, 0.0)
    t_d = eye_f + x
    for _ in range(3):
        x = _bmm(x, x)
        t_d = t_d + _bmm(t_d, x)
    m1 = _bmm(t_d, jnp.where(same_sub, 0.0, a_mat))
    m2 = _bmm(m1, m1)
    t = _bmm(eye_f - m1 + m2 - _bmm(m1, m2), t_d)
    e_g = jnp.exp(g_col)
    sol = _bmm(t, jnp.concatenate([v * b_col, kb * e_g], axis=2))
    g_last = g_row[:, :, CHUNK - 1:CHUNK]
    qd = q * e_g
    kt = k * jnp.exp(g_last - g_col)
    at = jnp.concatenate([attn, jnp.zeros_like(attn)], axis=2)
    for h in range(heads):
        hs = slice(h * LANES, (h + 1) * LANES)
        bs = slice(h * cb, (h + 1) * cb)
        u_ref[0, :, hs] = sol[bs, :, :LANES].reshape(rows, LANES)
        w_ref[0, :, hs] = sol[bs, :, LANES:].reshape(rows, LANES).astype(BF16)
        qd_ref[0, :, hs] = qd[bs].reshape(rows, LANES).astype(BF16)
        kt_ref[0, :, hs] = kt[bs].reshape(rows, LANES).astype(BF16)
        at_ref[0, :, hs] = at[bs].reshape(rows, LANES).astype(BF16)


def _wy(qkv_a, conv_w, beta5, gc5, batch, lp, heads):
    n_chunks = lp // CHUNK
    cb = max(c for c in range(1, 7) if n_chunks % c == 0)
    rows = cb * CHUNK
    width = heads * LANES
    ospec = pl.BlockSpec((1, rows, width), lambda b, i: (b, i, 0))
    rspec = pl.BlockSpec((1, 8, cb, 1, CHUNK), lambda b, i: (b, 0, i, 0, 0))
    f32o = jax.ShapeDtypeStruct((batch, lp, width), F32)
    b16o = jax.ShapeDtypeStruct((batch, lp, width), BF16)
    return pl.pallas_call(
        functools.partial(_wy_body, heads=heads, cb=cb),
        grid=(batch, n_chunks // cb),
        in_specs=[pl.BlockSpec((1, rows, 3 * width), lambda b, i: (b, i, 0)),
                  pl.BlockSpec((1, HALO_ROWS, 3 * width),
                               lambda b, i: (b, jnp.maximum(i * (rows // HALO_ROWS) - 1, 0), 0)),
                  pl.BlockSpec((CONV_K, 3 * width), lambda b, i: (0, 0)), rspec, rspec],
        out_specs=[ospec] * 5,
        out_shape=[f32o, b16o, b16o, b16o, b16o],
        compiler_params=_cparams(("parallel", "parallel")),
        name="delta_rule_wy",
    )(qkv_a, qkv_a, conv_w, beta5, gc5)


def _scan_body(u_ref, w_ref, qd_ref, kt_ref, at_ref, gc_ref, zs_ref, onw_ref, o_ref, s_s, *, heads):
    @pl.when(pl.program_id(1) == 0)
    def _():
        s_s[...] = jnp.zeros_like(s_s)

    hsl = [slice(h * LANES, (h + 1) * LANES) for h in range(heads)]
    for c in range(BLOCK // CHUNK):
        rs = slice(c * CHUNK, (c + 1) * CHUNK)
        s = [s_s[h] for h in range(heads)]
        ws = [jnp.dot(jnp.concatenate([w_ref[0, rs, hs], qd_ref[0, rs, hs]], axis=0), s[h].astype(BF16),
                      preferred_element_type=F32) for h, hs in enumerate(hsl)]
        vb = [(u_ref[0, rs, hs] - ws[h][:CHUNK]).astype(BF16) for h, hs in enumerate(hsl)]
        upd = [_tn_dot(kt_ref[0, rs, hs], vb[h]) for h, hs in enumerate(hsl)]
        o = [ws[h][CHUNK:] + jnp.dot(at_ref[0, rs, h * LANES:h * LANES + CHUNK], vb[h], preferred_element_type=F32)
             for h in range(heads)]
        for h, hs in enumerate(hsl):
            dec = jnp.exp(gc_ref[0, h, c][:, CHUNK - 1:CHUNK])
            s_s[h] = s[h] * dec + upd[h]
            on = o[h] * lax.rsqrt(jnp.mean(o[h] * o[h], axis=-1, keepdims=True) + EPS) * onw_ref[...]
            o_ref[0, rs, hs] = (on * zs_ref[0, rs, hs].astype(F32)).astype(o_ref.dtype)


def _scan(u, w, qd, kt, at, gc5, zs, onw, batch, lp, heads):
    seq = lp - BLOCK
    width = heads * LANES
    ispec = pl.BlockSpec((1, BLOCK, width), lambda b, i: (b, i, 0))
    xspec = pl.BlockSpec((1, BLOCK, width), lambda b, i: (b, jnp.maximum(i - 1, 0), 0))
    return pl.pallas_call(
        functools.partial(_scan_body, heads=heads),
        grid=(batch, lp // BLOCK),
        in_specs=[ispec] * 5 + [pl.BlockSpec((1, 8, BLOCK // CHUNK, 1, CHUNK), lambda b, i: (b, 0, i, 0, 0)),
                                xspec, pl.BlockSpec((1, LANES), lambda b, i: (0, 0))],
        out_specs=xspec,
        out_shape=jax.ShapeDtypeStruct((batch, seq, width), BF16),
        scratch_shapes=[pltpu.VMEM((heads, LANES, LANES), F32)],
        compiler_params=_cparams(("parallel", "arbitrary")),
        name="delta_rule_scan",
    )(u, w, qd, kt, at, gc5, zs, onw)


def _fox_body(q_ref, k_ref, v_ref, cum_ref, o_ref, *, tq, pad):
    i = pl.program_id(2)
    q0 = pl.multiple_of(BLOCK + i * tq, BLOCK)
    th = tq // 2
    q = q_ref[0, pl.ds(q0, tq), :]

    def tile(qx, k0, size, mask, carry):
        m, l, acc = carry
        s = _nt_dot(qx, k_ref[0, pl.ds(k0, size), :]) - cum_ref[0, :, pl.ds(k0, size)]
        if mask is not None:
            s = jnp.where(mask, s, NEG_INF)
        m_new = jnp.maximum(m, jnp.max(s, axis=1, keepdims=True))
        alpha = jnp.exp2(m - m_new)
        p = jnp.exp2(s - m_new)
        l = alpha * l + jnp.sum(p, axis=1, keepdims=True)
        acc = alpha * acc + jnp.dot(p.astype(BF16), v_ref[0, pl.ds(k0, size), :], preferred_element_type=F32)
        return m_new, l, acc

    carry = (jnp.full((tq, 1), NEG_INF, F32), jnp.zeros((tq, 1), F32), jnp.zeros((tq, LANES), F32))
    carry = tile(q, 0, BLOCK, lax.broadcasted_iota(I32, (tq, BLOCK), 1) >= pad, carry)
    carry = lax.fori_loop(0, i, lambda j, c: tile(q, pl.multiple_of(BLOCK + j * tq, BLOCK), tq, None, c), carry)
    m, l, acc = tile(q, q0, th, lax.broadcasted_iota(I32, (tq, th), 1) <= lax.broadcasted_iota(I32, (tq, th), 0), carry)
    o_ref[0, :th, :] = (acc[:th] / l[:th]).astype(o_ref.dtype)
    causal = lax.broadcasted_iota(I32, (th, th), 1) <= lax.broadcasted_iota(I32, (th, th), 0)
    m, l, acc = tile(q[th:], q0 + th, th, causal, (m[th:], l[th:], acc[th:]))
    o_ref[0, th:, :] = (acc / l).astype(o_ref.dtype)


def _fox(qkv, cumf3, batch, lp, heads, pad):
    seq = lp - BLOCK
    tq = _pick_tile(seq, 1024, 2 * BLOCK)
    kvspec = lambda off: pl.BlockSpec((1, lp, LANES), lambda b, h, i: (b, 0, off + h))
    return pl.pallas_call(
        functools.partial(_fox_body, tq=tq, pad=pad),
        grid=(batch, heads, seq // tq),
        in_specs=[kvspec(0), kvspec(heads), kvspec(2 * heads),
                  pl.BlockSpec((1, 1, lp), lambda b, h, i: (b * 8 + h, 0, 0))],
        out_specs=pl.BlockSpec((1, tq, LANES), lambda b, h, i: (b, i, h)),
        out_shape=jax.ShapeDtypeStruct((batch, seq, heads * LANES), BF16),
        compiler_params=_cparams(("parallel", "parallel", "arbitrary")),
        name="forgetting_attention",
    )(qkv, qkv, qkv, cumf3)


def _merge_body(oa_ref, ob_ref, g_ref, x_ref, wa_ref, wb_ref, wo_ref, nw_ref, res_ref, xn_ref):
    d = x_ref.shape[1]
    th = x_ref.shape[0] // 2
    halves = [slice(0, th), slice(th, 2 * th)]
    ya = [jnp.dot(oa_ref[r, :], wa_ref[...], preferred_element_type=F32) for r in halves]
    yb = [jnp.dot(ob_ref[r, :], wb_ref[...], preferred_element_type=F32) for r in halves]
    mix = []
    for c, r in enumerate(halves):
        g = g_ref[r, :].astype(F32)
        mix.append((g[:, :d] * ya[c] + g[:, d:] * yb[c]).astype(BF16))
    y = [jnp.dot(mix[c], wo_ref[...], preferred_element_type=F32) for c in range(2)]
    for c, r in enumerate(halves):
        res = x_ref[r, :] + y[c]
        res_ref[r, :] = res
        xn = res * lax.rsqrt(jnp.mean(res * res, axis=-1, keepdims=True) + EPS) * nw_ref[...]
        xn_ref[r, :] = xn.astype(xn_ref.dtype)


def _merge(oa, ob, gates, x2d, wa, wb, wo, nw):
    m, d = x2d.shape
    bw = oa.shape[1]
    tm = _pick_tile(m, 512, 128)
    row = lambda c: pl.BlockSpec((tm, c), lambda i: (i, 0))
    full = lambda r, c: pl.BlockSpec((r, c), lambda i: (0, 0))
    return pl.pallas_call(
        _merge_body,
        grid=(m // tm,),
        in_specs=[row(bw), row(bw), row(2 * d), row(d), full(bw, d), full(bw, d), full(d, d), full(1, d)],
        out_specs=[row(d), row(d)],
        out_shape=[jax.ShapeDtypeStruct((m, d), F32), jax.ShapeDtypeStruct((m, d), BF16)],
        compiler_params=_cparams(("parallel",)),
        name="merge_out_proj",
    )(oa, ob, gates, x2d, wa, wb, wo, nw)


_CAND_GROUPS = ((0, 0, 8, 8), (0, 8, 16, 8), (1, 0, 8, 8), (2, 0, 8, 5), (3, 0, 8, 4), (4, 0, 8, 3),
                (5, 0, 8, 2), (6, 0, 8, 2), (7, 0, 8, 2))


def _sort_network(n):
    pairs, p = [], 1
    while p < n:
        k = p
        while k >= 1:
            for j in range(k % p, n - k, 2 * k):
                for i in range(min(k, n - j - k)):
                    if (i + j) // (2 * p) == (i + j + k) // (2 * p):
                        pairs.append((i + j, i + j + k))
            k //= 2
        p *= 2
    return pairs


def _better(a, b):
    return (a[0] > b[0]) | ((a[0] == b[0]) & (a[1] < b[1]))


def _top_rows(groups, n, out_refs):
    depth = len(groups)
    size = 1
    while size < depth:
        size *= 2
    g = list(groups) + [None] * (size - depth)
    for i, j in _sort_network(size):
        if g[j] is None:
            continue
        if g[i] is None:
            g[i], g[j] = g[j], None
            continue
        take = _better(g[j], g[i])
        g[i], g[j] = (tuple(jnp.where(take, y, x) for x, y in zip(g[i], g[j])),
                      tuple(jnp.where(take, x, y) for x, y in zip(g[i], g[j])))
    g = g[:depth]
    worst = (-jnp.inf, 1e9)
    for kk in range(n):
        best = g[0]
        for shift in (4, 2, 1):
            other = tuple(pltpu.roll(x, shift, 0) for x in best)
            take = _better(other, best)
            best = tuple(jnp.where(take, y, x) for x, y in zip(best, other))
        for ref, x in zip(out_refs, best):
            ref[kk:kk + 1, :] = x[0:1]
        win = g[0][1] == best[1]
        for d in range(min(depth, n - kk - 1)):
            nxt = g[d + 1] if d + 1 < depth else tuple(jnp.full_like(x, w) for x, w in zip(g[d], worst + (0.0,) * 8))
            g[d] = tuple(jnp.where(win, y, x) for x, y in zip(g[d], nxt))


def _route_body(xn_ref, wqt_ref, keys_ref, g_ref, a_ref, b_ref, qt_s, gt_s, at_s, bt_s, v_s, i_s, sc_s, fl_s, ea_s,
                eb_s, *, heads, n_keys):
    tm = xn_ref.shape[0]
    qt_s[...] = _nt_dot(wqt_ref[...], xn_ref[...]).astype(BF16)

    def row_ids(shape):
        return lax.broadcasted_iota(I32, shape, 0) + lax.shift_right_logical(lax.broadcasted_iota(I32, shape, 1), 31)

    key_iota = row_ids((n_keys, LANES)).astype(F32)
    sub = row_ids((8, LANES))

    def head(h, carry):
        st = []
        for p in range(2):
            qhp = qt_s[pl.ds(pl.multiple_of(h * 2 * LANES + p * LANES, LANES), LANES), :]
            kk = keys_ref[pl.ds(pl.multiple_of((p * heads + h) * n_keys, n_keys), n_keys), :]
            st.append(jnp.dot(kk, qhp, preferred_element_type=F32))
        for c in range(tm // LANES):
            cs = slice(c * LANES, (c + 1) * LANES)
            for p in range(2):
                ps = slice(p * tm + c * LANES, p * tm + (c + 1) * LANES)
                _top_rows([(st[p][8 * d:8 * d + 8, cs], key_iota[8 * d:8 * d + 8]) for d in range(n_keys // 8)], TOPK,
                          (v_s.at[:, ps], i_s.at[:, ps]))
        for c in range(tm // LANES):
            cs = slice(c * LANES, (c + 1) * LANES)
            v1, v2 = v_s[:, cs], v_s[:, tm + c * LANES:tm + (c + 1) * LANES]
            i1, i2 = i_s[:, cs], i_s[:, tm + c * LANES:tm + (c + 1) * LANES]
            groups = []
            for (a, b0, b1, nvalid) in _CAND_GROUPS:
                groups.append((jnp.where(sub < nvalid, v1[a:a + 1] + v2[b0:b1], -jnp.inf),
                               (a * TOPK + b0 + sub).astype(F32), jnp.broadcast_to(i1[a:a + 1], (8, LANES)), i2[b0:b1]))
            groups.append((v1[8:16] + v2[0:1], ((8 + sub) * TOPK).astype(F32), i1[8:16],
                           jnp.broadcast_to(i2[0:1], (8, LANES))))
            _top_rows(groups, TOPK, (sc_s.at[:, cs], fl_s.at[:, cs], ea_s.at[:, cs], eb_s.at[:, cs]))
        sc = sc_s[...]
        e = jnp.exp(sc - sc[0:1])
        row = pl.ds(pl.multiple_of(h * TOPK, TOPK), TOPK)
        gt_s[row, :] = e / jnp.sum(e, axis=0, keepdims=True)
        at_s[row, :] = ea_s[...]
        bt_s[row, :] = eb_s[...]
        return carry

    lax.fori_loop(0, heads, head, 0)
    g_ref[...] = gt_s[...].T
    a_ref[...] = at_s[...].T
    b_ref[...] = bt_s[...].T


def _route(xn, wq_t, keys2d, heads, n_keys):
    t, d = xn.shape
    tm = _pick_tile(t, 512, LANES)
    hk = heads * TOPK
    out = jax.ShapeDtypeStruct((t, hk), F32)
    ospec = pl.BlockSpec((tm, hk), lambda i: (i, 0))
    tmp = lambda w: pltpu.VMEM((TOPK, w), F32)
    return pl.pallas_call(
        functools.partial(_route_body, heads=heads, n_keys=n_keys),
        grid=(t // tm,),
        in_specs=[pl.BlockSpec((tm, d), lambda i: (i, 0)),
                  pl.BlockSpec(wq_t.shape, lambda i: (0, 0)),
                  pl.BlockSpec(keys2d.shape, lambda i: (0, 0))],
        out_specs=[ospec, ospec, ospec],
        out_shape=[out, out, out],
        scratch_shapes=[pltpu.VMEM((wq_t.shape[0], tm), BF16)] + [pltpu.VMEM((hk, tm), F32) for _ in range(3)]
        + [tmp(2 * tm), tmp(2 * tm), tmp(tm), tmp(tm), tmp(tm), tmp(tm)],
        compiler_params=_cparams(("parallel",)),
        name="peer_route",
    )(xn, wq_t, keys2d)


def _gelu(x):
    return 0.5 * x * (1.0 + lax.erf(x * (2.0 ** -0.5)))


W_PITCH_PAD = 8
MXU_DIM = 256


W_GROUP = 32


def _expert_body(xn_ref, res_ref, g_ref, a_ref, b_ref, ut_ref, v_ref, o_ref, w_s, stage_s, acc_s, *, n_keys, n_steps):
    j = pl.program_id(1)
    tm = xn_ref.shape[0]
    te = ut_ref.shape[1]
    pitch = n_keys + W_PITCH_PAD

    @pl.when(j == 0)
    def _():
        acc_s[...] = jnp.zeros_like(acc_s)
        key = lax.broadcasted_iota(I32, (n_keys, g_ref.shape[1]), 0).astype(BF16)

        def group(gi, carry):
            t0 = pl.multiple_of(gi * W_GROUP, W_GROUP)
            for tl in range(W_GROUP):
                a_row = a_ref[pl.ds(t0 + tl, 1), :].astype(BF16)
                b_row = b_ref[pl.ds(t0 + tl, 1), :].astype(BF16)
                g_row = g_ref[pl.ds(t0 + tl, 1), :].astype(BF16)
                pt = jnp.where(key == a_row, g_row, jnp.zeros_like(g_row))
                qt = jnp.where(key == b_row, jnp.ones_like(g_row), jnp.zeros_like(g_row))
                stage_s[tl * pitch:tl * pitch + n_keys, :] = _nt_dot(pt, qt)
            for i1 in range(n_keys):
                w_s[i1, pl.ds(t0, W_GROUP), :] = stage_s[pl.ds(i1, W_GROUP, stride=pitch), :].astype(BF16)
            return carry

        lax.fori_loop(0, tm // W_GROUP, group, 0)

    n_slab = te // n_keys
    act = jnp.dot(xn_ref[...], ut_ref[...], preferred_element_type=F32)
    coef = []
    for s in range(n_slab):
        w_slab = w_s[j * n_slab + s].astype(F32)
        coef.append((_gelu(act[:, s * n_keys:(s + 1) * n_keys]) * w_slab).astype(BF16))
    acc_s[...] += jnp.dot(jnp.concatenate(coef, axis=1), v_ref[...], preferred_element_type=F32)

    @pl.when(j == pl.num_programs(1) - 1)
    def _():
        o_ref[...] = res_ref[...] + acc_s[...]


def _experts(xn, res, g, a, b, u_t, v, n_keys):
    t, d = xn.shape
    ne = v.shape[0]
    hk = g.shape[1]
    tm = _pick_tile(t, 512, 2 * LANES)
    te = _pick_tile(ne, 2048, MXU_DIM)
    tok = lambda c: pl.BlockSpec((tm, c), lambda i, j: (i, 0))
    n_steps = ne // te
    assert n_steps % 2 == 0 and (n_steps // 2) * (te // n_keys) == n_keys // 2
    return pl.pallas_call(
        functools.partial(_expert_body, n_keys=n_keys, n_steps=n_steps),
        grid=(t // tm, n_steps),
        in_specs=[tok(d), tok(d), tok(hk), tok(hk), tok(hk),
                  pl.BlockSpec((d, te), lambda i, j: (0, j)), pl.BlockSpec((te, d), lambda i, j: (j, 0))],
        out_specs=tok(d),
        out_shape=jax.ShapeDtypeStruct((t, d), F32),
        scratch_shapes=[pltpu.VMEM((n_keys, tm, n_keys), BF16),
                        pltpu.VMEM(((n_keys + W_PITCH_PAD) * W_GROUP, n_keys), F32),
                        pltpu.VMEM((tm, d), F32)],
        compiler_params=_cparams(("parallel", "arbitrary")),
        name="peer_experts",
    )(xn, res, g, a, b, u_t, v)


def _layer(x, meta_tokens, norm_mix, w_in, conv_w, a_log, dt_bias, o_norm_a, q_norm_b, k_norm_b, f_bias,
           w_branch, w_out, norm_ffn, peer_wq, peer_sub_keys, expert_u, expert_v):
    batch, seq, d = x.shape
    n_meta = meta_tokens.shape[0]
    pad = BLOCK - n_meta
    lp = BLOCK + seq
    ha, hb = a_log.shape[0], f_bias.shape[0]
    aqk = ha * LANES
    bw = hb * LANES
    assert ha <= 8 and hb <= 8 and seq % BLOCK == 0 and conv_w.shape == (CONV_K, 3 * aqk)
    assert w_in.shape[1] == 4 * aqk + 2 * ha + 3 * bw + hb + 2 * d
    c_z, c_b, c_a = 3 * aqk, 4 * aqk, 4 * aqk + ha
    c_qkvb = 4 * aqk + 2 * ha
    c_f = c_qkvb + 3 * bw
    c_g = c_f + hb

    x2d = x.reshape(batch * seq, d)
    hn_x = _rmsnorm(x2d, norm_mix, BF16)
    hn_m = _rmsnorm(meta_tokens, norm_mix, BF16)
    hn_p = jnp.concatenate([jnp.zeros((batch, pad, d), BF16), jnp.broadcast_to(hn_m[None], (batch, n_meta, d)),
                            hn_x.reshape(batch, seq, d)], axis=1).reshape(batch * lp, d)

    wb16 = w_in.astype(BF16)
    qkv_a = _proj(_proj_plain_body, hn_p, wb16[:, :c_z], BF16, _pick_tile(c_z, 1024, LANES), name="proj_qkv_a")
    zs = _proj(_proj_silu_body, hn_x, wb16[:, c_z:c_b], BF16, _pick_tile(aqk, 1024, LANES), name="proj_z")
    scale = LANES ** -0.5 * LOG2E
    nw = jnp.concatenate([jnp.tile(q_norm_b * scale, hb), jnp.tile(k_norm_b, hb), jnp.ones((bw,), F32)]).reshape(1, 3 * bw)
    qkv_b = _proj(_proj_qkvb_body, hn_p, wb16[:, c_qkvb:c_f], BF16, bw, extra=(nw,), name="proj_qkv_b")
    gates = _proj(_proj_sigmoid_body, hn_x, wb16[:, c_g:], BF16, _pick_tile(2 * d, 1024, LANES), name="proj_gates")

    def rows8(w):
        return jnp.pad(w.T, ((0, 8 - w.shape[1]), (0, 0)))

    w_small = jnp.concatenate([rows8(wb16[:, c_b:c_a]), rows8(wb16[:, c_a:c_qkvb]), rows8(wb16[:, c_f:c_g]),
                               jnp.zeros((SMALL_ROWS - 24, d), BF16)], axis=0)
    small_t = _proj_small_t(hn_p, w_small)

    def lanes8(p):
        return jnp.broadcast_to(jnp.pad(p.astype(F32), (0, 8 - p.shape[0]))[:, None], (8, LANES))

    beta, gc, cumf = _prep(small_t, lanes8(a_log), lanes8(dt_bias), lanes8(f_bias), batch, lp, pad)
    n_chunks = lp // CHUNK
    beta5 = beta.reshape(batch, 8, n_chunks, 1, CHUNK)
    gc5 = gc.reshape(batch, 8, n_chunks, 1, CHUNK)
    u, w, qd, kt, at = _wy(qkv_a.reshape(batch, lp, 3 * aqk), conv_w, beta5, gc5, batch, lp, ha)
    o_a = _scan(u, w, qd, kt, at, gc5, zs.reshape(batch, seq, aqk), o_norm_a.reshape(1, LANES).astype(F32),
                batch, lp, ha)

    o_b = _fox(qkv_b.reshape(batch, lp, 3 * bw), cumf.reshape(batch * 8, 1, lp), batch, lp, hb, pad)

    res, xn = _merge(o_a.reshape(batch * seq, aqk), o_b.reshape(batch * seq, bw), gates, x2d,
                     w_branch[0].astype(BF16), w_branch[1].astype(BF16), w_out.astype(BF16),
                     norm_ffn.reshape(1, d).astype(F32))

    hp, n_keys = peer_sub_keys.shape[1], peer_sub_keys.shape[2]
    assert peer_sub_keys.shape[3] == LANES and n_keys == LANES and peer_wq.shape[1] == hp * 2 * LANES
    g_w, a_k, b_k = _route(xn, peer_wq.T.astype(BF16), peer_sub_keys.reshape(2 * hp * n_keys, LANES).astype(BF16),
                           hp, n_keys)
    out = _experts(xn, res, g_w, a_k, b_k, expert_u.T.astype(BF16), expert_v.astype(BF16), n_keys)
    return out.reshape(batch, seq, d)


def kernel(x, meta_tokens, norm_mix, w_in, conv_w, a_log, dt_bias, o_norm_a, q_norm_b, k_norm_b, f_bias,
           w_branch, w_out, norm_ffn, peer_wq, peer_sub_keys, expert_u, expert_v):
    assert norm_mix.shape[0] == 1, "single-layer block"
    return _layer(x, meta_tokens, norm_mix[0], w_in[0], conv_w[0], a_log[0], dt_bias[0], o_norm_a[0],
                  q_norm_b[0], k_norm_b[0], f_bias[0], w_branch[0], w_out[0], norm_ffn[0], peer_wq[0],
                  peer_sub_keys[0], expert_u[0], expert_v[0])
```
